```python
import math
import jax, jax.numpy as jnp
from jax import lax
import numpy as np

D_MODEL = 1024
BATCH = 2
SEQ = 8192
DEPTH = 4
DEC_BATCH = 128
DEC_SEQ = 8
PAST_LEN = 2048
PAGE_SIZE = 128

N_MIXERS = 2
N_GLA_LAYERS = (DEPTH + 1) // 2
N_FOX_LAYERS = DEPTH // 2
MIX_WIDTH = D_MODEL
TOK_WIDTH = (3 * MIX_WIDTH) // 4
MEM_WIDTH = MIX_WIDTH - TOK_WIDTH
N_MEM = 256
MEM_HEADS = 4
MEM_HEAD_DIM = MEM_WIDTH // MEM_HEADS
GLA_HEADS = 4
GLA_DV = TOK_WIDTH // GLA_HEADS
GLA_DK = GLA_DV // 2
GLA_GATE_RANK = 16
GLA_TAU = 16.0
GLA_CHUNK = 64
FOX_HEAD_DIM = 64
FOX_HEADS = TOK_WIDTH // FOX_HEAD_DIM
FOX_BLOCK = 128
D_FF = 4 * D_MODEL
EPS = 1e-6
GLA_SPLITS = (GLA_HEADS * GLA_DK, GLA_HEADS * GLA_DK, TOK_WIDTH, GLA_GATE_RANK, TOK_WIDTH, MEM_WIDTH)
FOX_SPLITS = (TOK_WIDTH, TOK_WIDTH, TOK_WIDTH, FOX_HEADS, MEM_WIDTH)
GLA_IN = sum(GLA_SPLITS)
FOX_IN = sum(FOX_SPLITS)

kernel_name = "gla_fox_memory_hybrid_step"


def _split(x, sizes):
    idx = [int(i) for i in np.cumsum(sizes)[:-1]]
    return jnp.split(x, idx, axis=-1)


def rms_norm(x, g):
    xf = x.astype(jnp.float32)
    y = xf * lax.rsqrt(jnp.mean(xf * xf, axis=-1, keepdims=True) + EPS) * g.astype(jnp.float32)
    return y.astype(x.dtype)


def squared_relu_mlp(x, g, w_up, w_down):
    h = jnp.square(jax.nn.relu(rms_norm(x, g) @ w_up))
    return x + h @ w_down


def memory_kv(mem, g, w_kv, k_gain):
    b = mem.shape[0]
    k, v = jnp.split(rms_norm(mem, g) @ w_kv, 2, axis=-1)
    k = rms_norm(k.reshape(b, -1, MEM_HEADS, MEM_HEAD_DIM), k_gain)
    return k, v.reshape(b, -1, MEM_HEADS, MEM_HEAD_DIM)


def memory_attention(xq, mem_k, mem_v, q_gain):
    b, t, _ = xq.shape
    q = rms_norm(xq.reshape(b, t, MEM_HEADS, MEM_HEAD_DIM), q_gain)
    s = jnp.einsum('bthd,bmhd->bhtm', q, mem_k, preferred_element_type=jnp.float32) * (MEM_HEAD_DIM ** -0.5)
    p = jax.nn.softmax(s, axis=-1)
    o = jnp.einsum('bhtm,bmhd->bthd', p, mem_v.astype(jnp.float32))
    return o.reshape(b, t, MEM_WIDTH).astype(xq.dtype)


def gla_recurrence(q, k, v, log_a, s0):
    bsz, t, h, _ = q.shape
    dv = v.shape[-1]
    c = min(GLA_CHUNK, t)
    n = -(-t // c)
    pad = n * c - t

    def chunks(a):
        a = a.astype(jnp.float32)
        if pad:
            a = jnp.pad(a, ((0, 0), (0, pad), (0, 0), (0, 0)))
        return a.reshape(bsz, n, c, h, a.shape[-1]).transpose(1, 0, 3, 2, 4)

    causal = jnp.tril(jnp.ones((c, c), dtype=bool))

    def step(s, inp):
        qc, kc, vc, ac = inp
        b = jnp.cumsum(ac, axis=2)
        o_inter = jnp.einsum('bhtd,bhde->bhte', qc * jnp.exp(b), s)
        diff = b[:, :, :, None, :] - b[:, :, None, :, :]
        dec = jnp.exp(jnp.where(causal[:, :, None], diff, -jnp.inf))
        attn = jnp.sum(qc[:, :, :, None, :] * kc[:, :, None, :, :] * dec, axis=-1)
        o_intra = jnp.einsum('bhts,bhse->bhte', attn, vc)
        b_last = b[:, :, -1:, :]
        s = jnp.exp(b_last[:, :, 0, :, None]) * s + jnp.einsum('bhsd,bhse->bhde', kc * jnp.exp(b_last - b), vc)
        return s, o_inter + o_intra

    s_fin, o = lax.scan(step, s0.astype(jnp.float32), (chunks(q), chunks(k), chunks(v), chunks(log_a)))
    o = o.transpose(1, 0, 3, 2, 4).reshape(bsz, n * c, h, dv)[:, :t]
    return o.astype(v.dtype), s_fin


def gla_layer(x, s0, mem_k, mem_v, g_mix, w_in, w_a2, b_a, b_r, o_gain, w_out, mq_gain):
    bsz, t, _ = x.shape
    xn = rms_norm(x, g_mix)
    q, k, v, a_lr, r, xq = _split(xn @ w_in, GLA_SPLITS)
    log_a = jax.nn.log_sigmoid((a_lr @ w_a2 + b_a).astype(jnp.float32)) / GLA_TAU
    heads = lambda a, d: a.reshape(bsz, t, GLA_HEADS, d)
    o, s_new = gla_recurrence(heads(q, GLA_DK) * (GLA_DK ** -0.5), heads(k, GLA_DK), heads(v, GLA_DV),
                              heads(log_a, GLA_DK), s0)
    o = rms_norm(o, o_gain).reshape(bsz, t, TOK_WIDTH) * jax.nn.silu(r + b_r)
    xo = memory_attention(xq, mem_k, mem_v, mq_gain)
    return x + jnp.concatenate([o, xo], axis=-1) @ w_out, s_new


def fox_attention(q, k, v, c_q, c_k, pos_q, pos_k):
    bsz, t, h, d = q.shape
    blk = min(FOX_BLOCK, t)
    nb = -(-t // blk)
    pad = nb * blk - t
    if pad:
        q = jnp.pad(q, ((0, 0), (0, pad), (0, 0), (0, 0)))
        c_q = jnp.pad(c_q, ((0, 0), (0, pad), (0, 0)))
        pos_q = jnp.pad(pos_q, (0, pad), mode='edge')
    qb = q.reshape(bsz, nb, blk, h, d).swapaxes(0, 1)
    cb = c_q.reshape(bsz, nb, blk, h).swapaxes(0, 1)
    pb = pos_q.reshape(nb, blk)
    ck = jnp.swapaxes(c_k, 1, 2).astype(jnp.float32)
    vf = v.astype(jnp.float32)
    scale = d ** -0.5

    def block(args):
        qi, ci, pi = args
        s = jnp.einsum('bqhd,bkhd->bhqk', qi, k, preferred_element_type=jnp.float32) * scale
        s = s + jnp.swapaxes(ci, 1, 2)[..., None] - ck[:, :, None, :]
        s = jnp.where(pi[:, None] >= pos_k[None, :], s, -jnp.inf)
        p = jax.nn.softmax(s, axis=-1)
        return jnp.einsum('bhqk,bkhd->bqhd', p, vf)

    o = lax.map(block, (qb, cb, pb))
    o = o.swapaxes(0, 1).reshape(bsz, nb * blk, h, d)[:, :t]
    return o.reshape(bsz, t, h * d).astype(q.dtype)


def fox_layer(x, past, mem_k, mem_v, g_mix, w_in, b_f, q_gain, k_gain, w_out, mq_gain):
    bsz, t, _ = x.shape
    xn = rms_norm(x, g_mix)
    q, k, v, f_logit, xq = _split(xn @ w_in, FOX_SPLITS)
    heads = lambda a: a.reshape(bsz, t, FOX_HEADS, FOX_HEAD_DIM)
    q = rms_norm(heads(q), q_gain)
    k = rms_norm(heads(k), k_gain)
    v = heads(v)
    logf = jax.nn.log_sigmoid(f_logit.astype(jnp.float32) + b_f.astype(jnp.float32))
    if past is None:
        t0 = 0
        k_all, v_all, logf_all = k, v, logf
    else:
        k_past, v_past, logf_past = past
        t0 = k_past.shape[1]
        k_all = jnp.concatenate([k_past.astype(k.dtype), k], axis=1)
        v_all = jnp.concatenate([v_past.astype(v.dtype), v], axis=1)
        logf_all = jnp.concatenate([logf_past.astype(jnp.float32), logf], axis=1)
    c_all = jnp.cumsum(logf_all, axis=1)
    pos_k = jnp.arange(k_all.shape[1])
    o = fox_attention(q, k_all, v_all, c_all[:, t0:], c_all, t0 + jnp.arange(t), pos_k)
    xo = memory_attention(xq, mem_k, mem_v, mq_gain)
    return x + jnp.concatenate([o, xo], axis=-1) @ w_out, k, v, logf


def setup_inputs(seed: int = 0) -> dict:
    key = jax.random.key(seed)
    ks = iter(jax.random.split(key, 48))
    nrm = lambda shape, scale=1.0: jax.random.normal(next(ks), shape, jnp.float32) * scale
    gain = lambda shape: 1.0 + 0.02 * nrm(shape)
    n_pages = PAST_LEN // PAGE_SIZE
    n_used = DEC_BATCH * n_pages
    n_pool = n_used + max(1, n_used // 4)
    page_table = jax.random.permutation(next(ks), n_pool)[:n_used].reshape(DEC_BATCH, n_pages).astype(jnp.int32)
    return {
        'x_prompt': nrm((BATCH, SEQ, D_MODEL)),
        'x_sample': nrm((DEC_BATCH, DEC_SEQ, D_MODEL)),
        'mem_prompt': nrm((BATCH, N_MEM, D_MODEL)),
        'cache_fox_k': nrm((n_pool, N_FOX_LAYERS, PAGE_SIZE, FOX_HEADS, FOX_HEAD_DIM)),
        'cache_fox_v': nrm((n_pool, N_FOX_LAYERS, PAGE_SIZE, FOX_HEADS, FOX_HEAD_DIM)),
        'cache_fox_logf': jax.nn.log_sigmoid(2.0 + nrm((n_pool, N_FOX_LAYERS, PAGE_SIZE, FOX_HEADS))),
        'page_table': page_table,
        'state_gla': nrm((N_GLA_LAYERS, DEC_BATCH, GLA_HEADS, GLA_DK, GLA_DV), 0.5),
        'cache_mem_k': nrm((DEPTH, DEC_BATCH, N_MEM, MEM_HEADS, MEM_HEAD_DIM)),
        'cache_mem_v': nrm((DEPTH, DEC_BATCH, N_MEM, MEM_HEADS, MEM_HEAD_DIM)),
        'norm_mix': gain((DEPTH, D_MODEL)),
        'norm_mlp': gain((DEPTH, D_MODEL)),
        'norm_mem': gain((DEPTH, D_MODEL)),
        'gla_w_in': nrm((N_GLA_LAYERS, D_MODEL, GLA_IN), D_MODEL ** -0.5),
        'gla_w_a2': nrm((N_GLA_LAYERS, GLA_GATE_RANK, GLA_HEADS * GLA_DK), GLA_GATE_RANK ** -0.5),
        'gla_b_a': nrm((N_GLA_LAYERS, GLA_HEADS * GLA_DK), 0.1),
        'gla_b_r': nrm((N_GLA_LAYERS, TOK_WIDTH), 0.02),
        'gla_o_norm': gain((N_GLA_LAYERS, GLA_DV)),
        'gla_w_out': nrm((N_GLA_LAYERS, MIX_WIDTH, D_MODEL), MIX_WIDTH ** -0.5),
        'fox_w_in': nrm((N_FOX_LAYERS, D_MODEL, FOX_IN), D_MODEL ** -0.5),
        'fox_b_f': 2.0 + nrm((N_FOX_LAYERS, FOX_HEADS), 0.1),
        'fox_q_norm': gain((N_FOX_LAYERS, FOX_HEAD_DIM)),
        'fox_k_norm': gain((N_FOX_LAYERS, FOX_HEAD_DIM)),
        'fox_w_out': nrm((N_FOX_LAYERS, MIX_WIDTH, D_MODEL), MIX_WIDTH ** -0.5),
        'mem_w_kv': nrm((DEPTH, D_MODEL, 2 * MEM_WIDTH), D_MODEL ** -0.5),
        'mem_q_norm': gain((DEPTH, MEM_HEAD_DIM)),
        'mem_k_norm': gain((DEPTH, MEM_HEAD_DIM)),
        'mlp_w_up': nrm((DEPTH, D_MODEL, D_FF), D_MODEL ** -0.5),
        'mlp_w_down': nrm((DEPTH, D_FF, D_MODEL), D_FF ** -0.5),
    }


def reference(x_prompt, x_sample, mem_prompt, cache_fox_k, cache_fox_v, cache_fox_logf, page_table,
              state_gla, cache_mem_k, cache_mem_v, norm_mix, norm_mlp, norm_mem,
              gla_w_in, gla_w_a2, gla_b_a, gla_b_r, gla_o_norm, gla_w_out,
              fox_w_in, fox_b_f, fox_q_norm, fox_k_norm, fox_w_out,
              mem_w_kv, mem_q_norm, mem_k_norm, mlp_w_up, mlp_w_down):
    n_b = x_prompt.shape[0]
    n_db = x_sample.shape[0]
    yp, ys = x_prompt, x_sample
    fk_p, fv_p, fl_p, gs_p, mk_p_all, mv_p_all = [], [], [], [], [], []
    fk_s, fv_s, fl_s, gs_s = [], [], [], []
    for l in range(DEPTH):
        i = l // N_MIXERS
        mk_p, mv_p = memory_kv(mem_prompt, norm_mem[l], mem_w_kv[l], mem_k_norm[l])
        mk_p_all.append(mk_p)
        mv_p_all.append(mv_p)
        mk_s, mv_s = cache_mem_k[l], cache_mem_v[l]
        if l % N_MIXERS == 0:
            w = (norm_mix[l], gla_w_in[i], gla_w_a2[i], gla_b_a[i], gla_b_r[i], gla_o_norm[i], gla_w_out[i], mem_q_norm[l])
            s0 = jnp.zeros((n_b, GLA_HEADS, GLA_DK, GLA_DV), jnp.float32)
            yp, sp = gla_layer(yp, s0, mk_p, mv_p, *w)
            ys, ss = gla_layer(ys, state_gla[i], mk_s, mv_s, *w)
            gs_p.append(sp)
            gs_s.append(ss)
        else:
            w = (norm_mix[l], fox_w_in[i], fox_b_f[i], fox_q_norm[i], fox_k_norm[i], fox_w_out[i], mem_q_norm[l])
            yp, kp, vp, lp = fox_layer(yp, None, mk_p, mv_p, *w)
            k_past = cache_fox_k[page_table, i].reshape(n_db, -1, FOX_HEADS, FOX_HEAD_DIM)
            v_past = cache_fox_v[page_table, i].reshape(n_db, -1, FOX_HEADS, FOX_HEAD_DIM)
            l_past = cache_fox_logf[page_table, i].reshape(n_db, -1, FOX_HEADS)
            ys, ksn, vsn, lsn = fox_layer(ys, (k_past, v_past, l_past), mk_s, mv_s, *w)
            fk_p.append(kp)
            fv_p.append(vp)
            fl_p.append(lp)
            fk_s.append(ksn)
            fv_s.append(vsn)
            fl_s.append(lsn)
        yp = squared_relu_mlp(yp, norm_mlp[l], mlp_w_up[l], mlp_w_down[l])
        ys = squared_relu_mlp(ys, norm_mlp[l], mlp_w_up[l], mlp_w_down[l])
    fox_k_prompt = jnp.stack(fk_p, axis=1)
    fox_v_prompt = jnp.stack(fv_p, axis=1)
    fox_logf_prompt = jnp.stack(fl_p, axis=1)
    gla_state_prompt = jnp.stack(gs_p, axis=0)
    mem_k_prompt = jnp.stack(mk_p_all, axis=0)
    mem_v_prompt = jnp.stack(mv_p_all, axis=0)
    fox_k_sample = jnp.stack(fk_s, axis=1)
    fox_v_sample = jnp.stack(fv_s, axis=1)
    fox_logf_sample = jnp.stack(fl_s, axis=1)
    gla_state_sample = jnp.stack(gs_s, axis=0)
    return (yp, ys, fox_k_prompt, fox_v_prompt, fox_logf_prompt, gla_state_prompt, mem_k_prompt, mem_v_prompt,
            fox_k_sample, fox_v_sample, fox_logf_sample, gla_state_sample)
```

```python
import functools

import numpy as np
import jax
import jax.numpy as jnp
from jax import lax
from jax.experimental import pallas as pl
from jax.experimental.pallas import tpu as pltpu

F32 = jnp.float32
BF16 = jnp.bfloat16

D_MODEL = 1024
DEPTH = 4
PAGE_SIZE = 128
TOK_WIDTH = 768
MEM_WIDTH = 256
N_MEM = 256
MEM_HEADS = 4
MEM_HEAD_DIM = 64
GLA_HEADS = 4
GLA_DV = 192
GLA_DK = 96
GLA_GATE_RANK = 16
GLA_TAU = 16.0
FOX_HEAD_DIM = 64
FOX_HEADS = 12
D_FF = 4 * D_MODEL
EPS = 1e-6

LANES = 128
SUBLANES = 8
GLA_DK_PAD = 128
GLA_DV_PAD = 256
NEG_BIG = -1e30
VMEM_LIMIT = 56 * 1024 * 1024


def _cparams(sem):
    return pltpu.CompilerParams(dimension_semantics=sem, vmem_limit_bytes=VMEM_LIMIT)


def _resident(a):
    return pl.BlockSpec(a.shape, lambda i: (0,) * a.ndim, pipeline_mode=pl.Buffered(1))


def _dot(a, b):
    return jnp.dot(a, b, preferred_element_type=F32)


def _dot_nt(a, b):
    return lax.dot_general(a, b, (((1,), (1,)), ((), ())), preferred_element_type=F32)


def _dot_tn(a, b):
    return lax.dot_general(a, b, (((0,), (0,)), ((), ())), preferred_element_type=F32)


def _split3(x):
    hi = x.astype(BF16)
    r1 = x - hi.astype(F32)
    mid = r1.astype(BF16)
    lo = (r1 - mid.astype(F32)).astype(BF16)
    return hi, mid, lo


def _dot01(m01, x):
    hi, mid, lo = _split3(x)
    return (_dot(m01, lo) + _dot(m01, mid)) + _dot(m01, hi)


def _rms(x, g):
    return x * lax.rsqrt(jnp.mean(x * x, axis=-1, keepdims=True) + EPS) * g


def _log_sigmoid(x):
    return jnp.minimum(x, 0.0) - jnp.log(1.0 + jnp.exp(-jnp.abs(x)))


def _half_mask(shape):
    lane = lax.broadcasted_iota(jnp.int32, shape, len(shape) - 1)
    return (lane & FOX_HEAD_DIM) == 0


def _log2(n):
    k = int(n).bit_length() - 1
    assert 1 << k == n, n
    return k


def _ones_where(cond):
    return jnp.where(cond, 1.0, 0.0).astype(BF16)


def _head_rms_pairs(x, gain):
    n = x.shape[-1] // LANES
    outs = []
    for c in range(n):
        xc = x[:, c * LANES:(c + 1) * LANES]
        sq = xc * xc
        even = _half_mask(xc.shape)
        s_e = jnp.sum(jnp.where(even, sq, 0.0), axis=-1, keepdims=True)
        s_o = jnp.sum(jnp.where(even, 0.0, sq), axis=-1, keepdims=True)
        rs = jnp.where(even, lax.rsqrt(s_e / FOX_HEAD_DIM + EPS), lax.rsqrt(s_o / FOX_HEAD_DIM + EPS))
        outs.append(xc * rs)
    return jnp.concatenate(outs, axis=-1) * gain


GLA_COLS = (GLA_HEADS * GLA_DK_PAD, GLA_HEADS * GLA_DK_PAD, GLA_HEADS * GLA_DV_PAD,
            GLA_HEADS * GLA_DV_PAD, MEM_WIDTH, LANES)
GLA_OFF = tuple(int(v) for v in np.cumsum((0,) + GLA_COLS))


def _gla_proj_kernel(x_ref, g_ref, w_ref, wa_ref, ba_ref, q_ref, k_ref, v_ref, r_ref, xq_ref, la_ref):
    xn = _rms(x_ref[...], g_ref[...]).astype(BF16)
    y = _dot(xn, w_ref[...])
    o = GLA_OFF
    q_ref[...] = y[:, o[0]:o[1]] * (GLA_DK ** -0.5)
    k_ref[...] = y[:, o[1]:o[2]]
    v_ref[...] = y[:, o[2]:o[3]].astype(v_ref.dtype)
    r_ref[...] = y[:, o[3]:o[4]]
    xq_ref[...] = y[:, o[4]:o[5]]
    a = _dot(y[:, o[5]:o[6]].astype(BF16), wa_ref[...]) + ba_ref[...]
    la_ref[...] = _log_sigmoid(a) * (1.0 / GLA_TAU)


def _gla_proj(x, g, w, wa, ba, tm):
    m = x.shape[0]
    row = lambda n: pl.BlockSpec((tm, n), lambda i: (i, 0))
    full = _resident
    hk, hv = GLA_HEADS * GLA_DK_PAD, GLA_HEADS * GLA_DV_PAD
    out_shape = (jax.ShapeDtypeStruct((m, hk), F32), jax.ShapeDtypeStruct((m, hk), F32),
                 jax.ShapeDtypeStruct((m, hv), F32), jax.ShapeDtypeStruct((m, hv), F32),
                 jax.ShapeDtypeStruct((m, MEM_WIDTH), F32), jax.ShapeDtypeStruct((m, hk), F32))
    return pl.pallas_call(
        _gla_proj_kernel,
        grid=(m // tm,),
        in_specs=[row(D_MODEL), full(g), full(w), full(wa), full(ba)],
        out_specs=(row(hk), row(hk), row(hv), row(hv), row(MEM_WIDTH), row(hk)),
        out_shape=out_shape,
        compiler_params=_cparams(("parallel",)),
    )(x, g, w, wa, ba)


FOX_COLS = (TOK_WIDTH, TOK_WIDTH, TOK_WIDTH, MEM_WIDTH, LANES)
FOX_OFF = tuple(int(v) for v in np.cumsum((0,) + FOX_COLS))


def _fox_proj_kernel(x_ref, g_ref, w_ref, qg_ref, kg_ref, bf_ref,
                     qb_ref, kf_ref, kb_ref, vf_ref, vb_ref, xq_ref, lf_ref):
    xn = _rms(x_ref[...], g_ref[...]).astype(BF16)
    y = _dot(xn, w_ref[...])
    o = FOX_OFF
    q = _head_rms_pairs(y[:, o[0]:o[1]], qg_ref[...])
    qb_ref[...] = (q * (FOX_HEAD_DIM ** -0.5)).astype(BF16)
    k = _head_rms_pairs(y[:, o[1]:o[2]], kg_ref[...])
    kf_ref[...] = k
    kb_ref[...] = k.astype(BF16)
    v = y[:, o[2]:o[3]]
    vf_ref[...] = v
    vb_ref[...] = v.astype(BF16)
    xq_ref[...] = y[:, o[3]:o[4]]
    lf_ref[...] = _log_sigmoid(y[:, o[4]:o[5]] + bf_ref[...])


def _fox_proj(x, g, w, qg, kg, bfp, tm):
    m = x.shape[0]
    row = lambda n: pl.BlockSpec((tm, n), lambda i: (i, 0))
    full = _resident
    tw = TOK_WIDTH
    out_shape = (jax.ShapeDtypeStruct((m, tw), BF16), jax.ShapeDtypeStruct((m, tw), F32),
                 jax.ShapeDtypeStruct((m, tw), BF16), jax.ShapeDtypeStruct((m, tw), F32),
                 jax.ShapeDtypeStruct((m, tw), BF16), jax.ShapeDtypeStruct((m, MEM_WIDTH), F32),
                 jax.ShapeDtypeStruct((m, LANES), F32))
    return pl.pallas_call(
        _fox_proj_kernel,
        grid=(m // tm,),
        in_specs=[row(D_MODEL), full(g), full(w), full(qg), full(kg), full(bfp)],
        out_specs=(row(tw), row(tw), row(tw), row(tw), row(tw), row(MEM_WIDTH), row(LANES)),
        out_shape=out_shape,
        compiler_params=_cparams(("parallel",)),
    )(x, g, w, qg, kg, bfp)


def _mem_kv_kernel(x_ref, g_ref, w_ref, kg_ref, k_ref, v_ref):
    xn = _rms(x_ref[...], g_ref[...]).astype(BF16)
    y = _dot(xn, w_ref[...])
    k_ref[...] = _head_rms_pairs(y[:, :MEM_WIDTH], kg_ref[...])
    v_ref[...] = y[:, MEM_WIDTH:]


def _mem_kv(x, g, w, kg):
    m = x.shape[0]
    full = _resident
    out = jax.ShapeDtypeStruct((m, MEM_WIDTH), F32)
    return pl.pallas_call(
        _mem_kv_kernel,
        grid=(1,),
        in_specs=[full(x), full(g), full(w), full(kg)],
        out_specs=(pl.BlockSpec((m, MEM_WIDTH), lambda i: (0, 0)),) * 2,
        out_shape=(out, out),
        compiler_params=_cparams(("arbitrary",)),
    )(x, g, w, kg)


def _gla_levels(c):
    out, h = [], c // 2
    while h >= SUBLANES:
        out.append(h)
        h //= 2
    return out


def _gla_cum_matrix(c):
    t = np.arange(c)
    blocks = [(t[None, :] <= t[:, None])]
    for h in _gla_levels(c):
        anchor = (t // (2 * h)) * (2 * h) + h - 1
        blocks.append(t[None, :] <= anchor[:, None])
    return jnp.asarray(np.concatenate(blocks, axis=0).astype(np.float32), dtype=BF16)


def _gla_kernel(*refs, chunk, n_sub, t_valid, has_s0):
    if has_s0:
        cum_ref, q_ref, k_ref, la_ref, v_ref, r_ref, og_ref, br_ref, s0_ref, o_ref, sout_ref, st_ref, pad_ref = refs
    else:
        cum_ref, q_ref, k_ref, la_ref, v_ref, r_ref, og_ref, br_ref, o_ref, sout_ref, st_ref, pad_ref = refs
        s0_ref = None
    c = chunk
    levels = _gla_levels(c)
    nb = c // SUBLANES
    step = pl.program_id(2)

    @pl.when(step == 0)
    def _():
        if has_s0:
            pad_ref[...] = jnp.zeros(pad_ref.shape, F32)
            pad_ref[:GLA_DK, :GLA_DV] = s0_ref[0, 0]
            st_ref[...] = pad_ref[...].T
        else:
            st_ref[...] = jnp.zeros(st_ref.shape, F32)

    def load(ref, rows, dtype=F32):
        x = ref[0, rows, :].astype(dtype)
        if t_valid < c:
            x = jnp.concatenate([x, jnp.zeros((c - t_valid, x.shape[-1]), dtype)], axis=0)
        return x

    def do_chunk(rows):
        q = load(q_ref, rows)
        k = load(k_ref, rows)
        la = load(la_ref, rows)
        v = load(v_ref, rows)
        vb = v.astype(BF16)
        cum = _dot01(cum_ref[...], la)
        b = cum[:c]
        b_last = b[c - 1:c, :]
        st = st_ref[...]
        o = _dot_nt((q * jnp.exp(b)).astype(BF16), st.astype(BF16))
        if levels:
            ti = lax.broadcasted_iota(jnp.int32, (c, c), 0)
            si = lax.broadcasted_iota(jnp.int32, (c, c), 1)
            row = lax.broadcasted_iota(jnp.int32, (c, GLA_DK_PAD), 0)
            attn = jnp.zeros((c, c), F32)
            for li, h in enumerate(levels):
                anchor = cum[(li + 1) * c:(li + 2) * c]
                gdec = jnp.exp(-jnp.abs(b - anchor))
                first = (row & h) == 0
                ql = jnp.where(first, 0.0, q * gdec).astype(BF16)
                kl = jnp.where(first, k * gdec, 0.0).astype(BF16)
                a = _dot_nt(ql, kl)
                sh = _log2(2 * h)
                attn = attn + jnp.where((ti >> sh) == (si >> sh), a, 0.0)
            o = o + _dot(attn.astype(BF16), vb)
        q3 = q.reshape(nb, SUBLANES, GLA_DK_PAD)
        k3 = k.reshape(nb, SUBLANES, GLA_DK_PAD)
        b3 = b.reshape(nb, SUBLANES, GLA_DK_PAD)
        v3 = v.reshape(nb, SUBLANES, GLA_DV_PAD)
        sub = lax.broadcasted_iota(jnp.int32, (nb, SUBLANES, GLA_DK_PAD), 1)
        od = jnp.zeros((nb, SUBLANES, GLA_DV_PAD), F32)
        for j in range(SUBLANES):
            kj = jnp.broadcast_to(k3[:, j:j + 1, :], k3.shape)
            bj = jnp.broadcast_to(b3[:, j:j + 1, :], b3.shape)
            e = jnp.exp(jnp.where(sub >= j, b3 - bj, -jnp.inf))
            col = jnp.sum(q3 * kj * e, axis=-1, keepdims=True)
            od = od + col * jnp.broadcast_to(v3[:, j:j + 1, :], v3.shape)
        o = o + od.reshape(c, GLA_DV_PAD)
        kg = (k * jnp.exp(b_last - b)).astype(BF16)
        st_ref[...] = st * jnp.exp(b_last) + _dot_tn(vb, kg)
        ms = jnp.sum(o * o, axis=-1, keepdims=True) * (1.0 / GLA_DV)
        on = o * lax.rsqrt(ms + EPS) * og_ref[...]
        r = load(r_ref, rows) + br_ref[...]
        res = on * (r * (1.0 / (1.0 + jnp.exp(-r))))
        return res[:t_valid] if t_valid < c else res

    if n_sub == 1:
        o_ref[0] = do_chunk(slice(None)).astype(o_ref.dtype)
    else:
        def body(i, carry):
            rows = pl.ds(pl.multiple_of(i * c, c), c)
            o_ref[0, rows, :] = do_chunk(rows).astype(o_ref.dtype)
            return carry
        lax.fori_loop(0, n_sub, body, 0)

    @pl.when(step == pl.num_programs(2) - 1)
    def _():
        pad_ref[...] = st_ref[...].T
        sout_ref[0, 0] = pad_ref[:GLA_DK, :GLA_DV]


def _gla_mix(q, k, la, v, r, og, br, s0, *, chunk, tblk):
    bsz, t, _ = q.shape
    has_s0 = s0 is not None
    if t < chunk:
        t_valid, n_sub, tblk = t, 1, t
    else:
        t_valid, n_sub = chunk, tblk // chunk
    cum = _gla_cum_matrix(chunk)
    kblk = pl.BlockSpec((1, tblk, GLA_DK_PAD), lambda b, h, i: (b, i, h))
    vblk = pl.BlockSpec((1, tblk, GLA_DV_PAD), lambda b, h, i: (b, i, h))
    sblk = pl.BlockSpec((1, 1, GLA_DK, GLA_DV), lambda b, h, i: (b, h, 0, 0))
    in_specs = [pl.BlockSpec(cum.shape, lambda b, h, i: (0, 0)), kblk, kblk, kblk, vblk, vblk,
                pl.BlockSpec((1, GLA_DV_PAD), lambda b, h, i: (0, 0)),
                pl.BlockSpec((1, GLA_DV_PAD), lambda b, h, i: (0, h))]
    args = [cum, q, k, la, v, r, og, br]
    if has_s0:
        in_specs.append(sblk)
        args.append(s0)
    kern = functools.partial(_gla_kernel, chunk=chunk, n_sub=n_sub, t_valid=t_valid, has_s0=has_s0)
    return pl.pallas_call(
        kern,
        grid=(bsz, GLA_HEADS, t // tblk),
        in_specs=in_specs,
        out_specs=(vblk, sblk),
        out_shape=(jax.ShapeDtypeStruct((bsz, t, GLA_HEADS * GLA_DV_PAD), BF16 if t >= chunk else F32),
                   jax.ShapeDtypeStruct((bsz, GLA_HEADS, GLA_DK, GLA_DV), F32)),
        scratch_shapes=[pltpu.VMEM((GLA_DV_PAD, GLA_DK_PAD), F32), pltpu.VMEM((GLA_DK_PAD, GLA_DV_PAD), F32)],
        compiler_params=_cparams(("parallel", "parallel", "arbitrary")),
    )(*args)


def _cumsum_kernel(tri_ref, x_ref, c_ref, carry_ref):
    @pl.when(pl.program_id(1) == 0)
    def _():
        carry_ref[...] = jnp.zeros(carry_ref.shape, F32)
    c = _dot01(tri_ref[...], x_ref[0]) + carry_ref[...]
    c_ref[0] = c
    carry_ref[...] = c[c.shape[0] - 1:, :]


def _cumsum_rows(x, tc):
    bsz, t, n = x.shape
    tri = jnp.asarray(np.tril(np.ones((tc, tc), np.float32)), dtype=BF16)
    blk = pl.BlockSpec((1, tc, n), lambda b, i: (b, i, 0))
    return pl.pallas_call(
        _cumsum_kernel,
        grid=(bsz, t // tc),
        in_specs=[pl.BlockSpec((tc, tc), lambda b, i: (0, 0)), blk],
        out_specs=blk,
        out_shape=jax.ShapeDtypeStruct((bsz, t, n), F32),
        scratch_shapes=[pltpu.VMEM((1, n), F32)],
        compiler_params=_cparams(("parallel", "arbitrary")),
    )(tri, x)


def _fox_flash_kernel(q_ref, k_ref, v_ref, c_ref, ct_ref, o_ref, m_ref, l_ref, cq_ref, acc_ref, *, tq, tk):
    pair = pl.program_id(1)
    qi = pl.program_id(2)
    kj = pl.program_id(3)
    nk = pl.num_programs(3)
    last = ((qi + 1) * tq - 1) // tk

    @pl.when(kj == 0)
    def _():
        m_ref[...] = jnp.full(m_ref.shape, NEG_BIG, F32)
        l_ref[...] = jnp.zeros(l_ref.shape, F32)
        acc_ref[...] = jnp.zeros(acc_ref.shape, F32)
        cb = c_ref[0]
        lane = lax.broadcasted_iota(jnp.int32, cb.shape, 1)
        for hh in range(2):
            cq_ref[hh] = jnp.sum(jnp.where(lane == 2 * pair + hh, cb, 0.0), axis=-1, keepdims=True)

    def update(masked):
        q = q_ref[0]
        k = k_ref[0]
        v = v_ref[0]
        even = _half_mask(q.shape)
        zero = jnp.zeros(q.shape, q.dtype)
        qs = (jnp.where(even, q, zero), jnp.where(even, zero, q))
        if masked:
            rowi = qi * tq + lax.broadcasted_iota(jnp.int32, (tq, tk), 0)
            coli = kj * tk + lax.broadcasted_iota(jnp.int32, (tq, tk), 1)
            keep = rowi >= coli
        alphas, pvs = [], []
        for hh in range(2):
            s = _dot_nt(qs[hh], k) + cq_ref[hh] - ct_ref[0, pl.ds(2 * pair + hh, 1), :]
            if masked:
                s = jnp.where(keep, s, NEG_BIG)
            m_old = m_ref[hh]
            m_new = jnp.maximum(m_old, jnp.max(s, axis=-1, keepdims=True))
            alpha = jnp.exp(m_old - m_new)
            p = jnp.exp(s - m_new)
            l_ref[hh] = alpha * l_ref[hh] + jnp.sum(p, axis=-1, keepdims=True)
            m_ref[hh] = m_new
            alphas.append(alpha)
            pvs.append(_dot(p.astype(BF16), v))
        even_o = _half_mask(acc_ref.shape)
        acc_ref[...] = (jnp.where(even_o, alphas[0], alphas[1]) * acc_ref[...]
                        + jnp.where(even_o, pvs[0], pvs[1]))

    crosses = (kj + 1) * tk - 1 > qi * tq

    @pl.when(jnp.logical_and(kj <= last, crosses))
    def _():
        update(True)

    @pl.when(jnp.logical_and(kj <= last, jnp.logical_not(crosses)))
    def _():
        update(False)

    @pl.when(kj == nk - 1)
    def _():
        even_o = _half_mask(acc_ref.shape)
        inv = jnp.where(even_o, 1.0 / l_ref[0], 1.0 / l_ref[1])
        o_ref[0] = (acc_ref[...] * inv).astype(o_ref.dtype)


def _fox_flash(q, k, v, c, ct, tq, tk):
    bsz, t, _ = q.shape
    nq, nk = t // tq, t // tk
    npair = FOX_HEADS // 2

    def kv_idx(b, p, i, j):
        return (b, jnp.minimum(j, ((i + 1) * tq - 1) // tk), p)

    def ct_idx(b, p, i, j):
        return (b, 0, jnp.minimum(j, ((i + 1) * tq - 1) // tk))

    kern = functools.partial(_fox_flash_kernel, tq=tq, tk=tk)
    return pl.pallas_call(
        kern,
        grid=(bsz, npair, nq, nk),
        in_specs=[pl.BlockSpec((1, tq, LANES), lambda b, p, i, j: (b, i, p)),
                  pl.BlockSpec((1, tk, LANES), kv_idx),
                  pl.BlockSpec((1, tk, LANES), kv_idx),
                  pl.BlockSpec((1, tq, LANES), lambda b, p, i, j: (b, i, 0)),
                  pl.BlockSpec((1, ct.shape[1], tk), ct_idx)],
        out_specs=pl.BlockSpec((1, tq, LANES), lambda b, p, i, j: (b, i, p)),
        out_shape=jax.ShapeDtypeStruct((bsz, t, TOK_WIDTH), BF16),
        scratch_shapes=[pltpu.VMEM((2, tq, 1), F32), pltpu.VMEM((2, tq, 1), F32),
                        pltpu.VMEM((2, tq, 1), F32), pltpu.VMEM((tq, LANES), F32)],
        compiler_params=_cparams(("parallel", "parallel", "parallel", "arbitrary")),
    )(q, k, v, c, ct)


def _head_rows(x, heads):
    lane = lax.broadcasted_iota(jnp.int32, x.shape, 1) >> _log2(FOX_HEAD_DIM)
    zero = jnp.zeros(x.shape, x.dtype)
    return jnp.concatenate([jnp.where(lane == h, x, zero) for h in range(heads)], axis=0)


def _head_diag(z, heads, t):
    lane = lax.broadcasted_iota(jnp.int32, (t, z.shape[1]), 1) >> _log2(FOX_HEAD_DIM)
    out = jnp.zeros((t, z.shape[1]), z.dtype)
    for h in range(heads):
        out = jnp.where(lane == h, z[h * t:(h + 1) * t, :], out)
    return out


def _fox_decode_kernel(pt_ref, q_ref, kn_ref, vn_ref, lfn_ref, kc_ref, vc_ref, lc_ref, o_ref,
                       wq_ref, m_ref, l_ref, acc_ref, carry_ref, cnk_ref, *, t_new):
    del pt_ref
    p = pl.program_id(1)
    npg = pl.num_programs(1)
    heads = FOX_HEADS
    rows = heads * t_new
    tpad = 2 * SUBLANES

    @pl.when(p == 0)
    def _():
        wq_ref[...] = _head_rows(q_ref[0], heads).astype(BF16)
        m_ref[...] = jnp.full(m_ref.shape, NEG_BIG, F32)
        l_ref[...] = jnp.zeros(l_ref.shape, F32)
        acc_ref[...] = jnp.zeros(acc_ref.shape, F32)
        carry_ref[...] = jnp.zeros(carry_ref.shape, F32)
        lfn = jnp.concatenate([lfn_ref[0], jnp.zeros((tpad - t_new, LANES), F32)], axis=0)
        ti = lax.broadcasted_iota(jnp.int32, (tpad, tpad), 0)
        si = lax.broadcasted_iota(jnp.int32, (tpad, tpad), 1)
        cs = _dot01(_ones_where(si <= ti), lfn)
        rh = lax.broadcasted_iota(jnp.int32, (rows, LANES), 0) >> _log2(t_new)
        lh = lax.broadcasted_iota(jnp.int32, (rows, LANES), 1)
        hsel = _ones_where(rh == lh)
        hi, mid, lo = _split3(cs)
        cnk_ref[...] = (_dot_nt(hsel, lo) + _dot_nt(hsel, mid)) + _dot_nt(hsel, hi)

    def online(s, v):
        m_old = m_ref[...]
        m_new = jnp.maximum(m_old, jnp.max(s, axis=-1, keepdims=True))
        alpha = jnp.exp(m_old - m_new)
        pr = jnp.exp(s - m_new)
        l_ref[...] = alpha * l_ref[...] + jnp.sum(pr, axis=-1, keepdims=True)
        m_ref[...] = m_new
        acc_ref[...] = alpha * acc_ref[...] + _dot(pr.astype(BF16), v)

    cnk = cnk_ref[...]
    ri = lax.broadcasted_iota(jnp.int32, cnk.shape, 0) & (t_new - 1)
    ci = lax.broadcasted_iota(jnp.int32, cnk.shape, 1)
    cn_col = jnp.sum(jnp.where(ri == ci, cnk, 0.0), axis=-1, keepdims=True)

    lp = lc_ref[0, 0]
    ji = lax.broadcasted_iota(jnp.int32, (PAGE_SIZE, PAGE_SIZE), 0)
    si2 = lax.broadcasted_iota(jnp.int32, (PAGE_SIZE, PAGE_SIZE), 1)
    suffix = _dot01_rhs(lp, _ones_where(ji > si2)) + carry_ref[...]
    carry_ref[...] = carry_ref[...] + jnp.sum(lp, axis=-1, keepdims=True)
    r96 = jnp.concatenate([jnp.broadcast_to(suffix[h:h + 1, :], (t_new, PAGE_SIZE)) for h in range(heads)],
                          axis=0)
    s = _dot_nt(wq_ref[...], kc_ref[0, 0].astype(BF16)) + r96 + cn_col
    online(s, vc_ref[0, 0].astype(BF16))

    @pl.when(p == npg - 1)
    def _():
        zpad = jnp.zeros((tpad - t_new, TOK_WIDTH), F32)
        kn = jnp.concatenate([kn_ref[0], zpad], axis=0).astype(BF16)
        vn = jnp.concatenate([vn_ref[0], zpad], axis=0).astype(BF16)
        sn = _dot_nt(wq_ref[...], kn) + cn_col - cnk
        sn = jnp.where(ci <= ri, sn, NEG_BIG)
        online(sn, vn)
        o_ref[0] = _head_diag(acc_ref[...] / l_ref[...], heads, t_new).astype(o_ref.dtype)


def _dot01_rhs(x, m01):
    hi, mid, lo = _split3(x)
    return (_dot(lo, m01) + _dot(mid, m01)) + _dot(hi, m01)


def _fox_decode(q, kn, vn, lfn, cache_k, cache_v, cache_lt, page_table, layer):
    nseq, t_new, _ = q.shape
    npg = page_table.shape[1]
    rows = FOX_HEADS * t_new
    seq = lambda n, d=None: pl.BlockSpec((1, t_new, n), lambda b, p, pt: (b, 0, 0))

    def page_idx(b, p, pt):
        return (pt[b, npg - 1 - p], layer, 0, 0)

    kern = functools.partial(_fox_decode_kernel, t_new=t_new)
    grid_spec = pltpu.PrefetchScalarGridSpec(
        num_scalar_prefetch=1,
        grid=(nseq, npg),
        in_specs=[seq(TOK_WIDTH), seq(TOK_WIDTH), seq(TOK_WIDTH), seq(LANES),
                  pl.BlockSpec((1, 1, PAGE_SIZE, TOK_WIDTH), page_idx),
                  pl.BlockSpec((1, 1, PAGE_SIZE, TOK_WIDTH), page_idx),
                  pl.BlockSpec((1, 1, 2 * SUBLANES, PAGE_SIZE), page_idx)],
        out_specs=seq(TOK_WIDTH),
        scratch_shapes=[pltpu.VMEM((rows, TOK_WIDTH), BF16), pltpu.VMEM((rows, 1), F32),
                        pltpu.VMEM((rows, 1), F32), pltpu.VMEM((rows, TOK_WIDTH), F32),
                        pltpu.VMEM((2 * SUBLANES, 1), F32), pltpu.VMEM((rows, 2 * SUBLANES), F32)],
    )
    return pl.pallas_call(
        kern,
        grid_spec=grid_spec,
        out_shape=jax.ShapeDtypeStruct((nseq, t_new, TOK_WIDTH), F32),
        compiler_params=_cparams(("parallel", "arbitrary")),
    )(page_table, q, kn, vn, lfn, cache_k, cache_v, cache_lt)


def _mem_attn_prompt_kernel(xq_ref, mk_ref, mv_ref, g_ref, o_ref):
    q = _head_rms_pairs(xq_ref[0], g_ref[...]) * (MEM_HEAD_DIM ** -0.5)
    outs = []
    for pr in range(MEM_WIDTH // LANES):
        cols = slice(pr * LANES, (pr + 1) * LANES)
        qp = q[:, cols].astype(BF16)
        kp = mk_ref[0][:, cols].astype(BF16)
        vp = mv_ref[0][:, cols].astype(BF16)
        even = _half_mask(qp.shape)
        zero = jnp.zeros(qp.shape, BF16)
        res = []
        for qh in (jnp.where(even, qp, zero), jnp.where(even, zero, qp)):
            s = _dot_nt(qh, kp)
            e = jnp.exp(s - jnp.max(s, axis=-1, keepdims=True))
            pv = _dot(e.astype(BF16), vp)
            res.append(pv / jnp.sum(e, axis=-1, keepdims=True))
        outs.append(jnp.where(even, res[0], res[1]))
    o_ref[0] = jnp.concatenate(outs, axis=-1).astype(o_ref.dtype)


def _mem_attn_prompt(xq, mk, mv, g, tq):
    bsz, t, _ = xq.shape
    qblk = pl.BlockSpec((1, tq, MEM_WIDTH), lambda b, i: (b, i, 0))
    mblk = pl.BlockSpec((1, N_MEM, MEM_WIDTH), lambda b, i: (b, 0, 0))
    return pl.pallas_call(
        _mem_attn_prompt_kernel,
        grid=(bsz, t // tq),
        in_specs=[qblk, mblk, mblk, pl.BlockSpec((1, MEM_WIDTH), lambda b, i: (0, 0))],
        out_specs=qblk,
        out_shape=jax.ShapeDtypeStruct((bsz, t, MEM_WIDTH), BF16),
        compiler_params=_cparams(("parallel", "parallel")),
    )(xq, mk, mv, g)


def _mem_attn_decode_kernel(xq_ref, mk_ref, mv_ref, g_ref, o_ref, *, t_new):
    q = _head_rms_pairs(xq_ref[0], g_ref[...]) * (MEM_HEAD_DIM ** -0.5)
    wq = _head_rows(q, MEM_HEADS).astype(BF16)
    s = _dot_nt(wq, mk_ref[0, 0].astype(BF16))
    e = jnp.exp(s - jnp.max(s, axis=-1, keepdims=True))
    z = _dot(e.astype(BF16), mv_ref[0, 0].astype(BF16)) / jnp.sum(e, axis=-1, keepdims=True)
    o_ref[0] = _head_diag(z, MEM_HEADS, t_new).astype(o_ref.dtype)


def _mem_attn_decode(xq, cache_mk, cache_mv, g, layer):
    nseq, t_new, _ = xq.shape
    qblk = pl.BlockSpec((1, t_new, MEM_WIDTH), lambda b: (b, 0, 0))
    mblk = pl.BlockSpec((1, 1, N_MEM, MEM_WIDTH), lambda b: (layer, b, 0, 0))
    return pl.pallas_call(
        functools.partial(_mem_attn_decode_kernel, t_new=t_new),
        grid=(nseq,),
        in_specs=[qblk, mblk, mblk, pl.BlockSpec((1, MEM_WIDTH), lambda b: (0, 0))],
        out_specs=qblk,
        out_shape=jax.ShapeDtypeStruct((nseq, t_new, MEM_WIDTH), F32),
        compiler_params=_cparams(("parallel",)),
    )(xq, cache_mk, cache_mv, g)


FF_CHUNK = 1024


def _out_mlp_kernel(x_ref, o_ref, xo_ref, wo_ref, wm_ref, g_ref, wu_ref, wd_ref, y_ref):
    y_ref[...] = (x_ref[...] + _dot(o_ref[...].astype(BF16), wo_ref[...])
                  + _dot(xo_ref[...].astype(BF16), wm_ref[...]))
    y1 = y_ref[...]
    xn = _rms(y1, g_ref[...]).astype(BF16)
    acc = y1
    for c in range(D_FF // FF_CHUNK):
        cols = slice(c * FF_CHUNK, (c + 1) * FF_CHUNK)
        h = jnp.maximum(_dot(xn, wu_ref[:, cols]), 0.0)
        acc = acc + _dot((h * h).astype(BF16), wd_ref[cols, :])
    y_ref[...] = acc


def _out_mlp(x, o, xo, wo, wm, g, wu, wd, tm):
    m = x.shape[0]
    row = lambda n: pl.BlockSpec((tm, n), lambda i: (i, 0))
    full = lambda a: pl.BlockSpec(a.shape, lambda i: (0,) * a.ndim, pipeline_mode=pl.Buffered(1))
    return pl.pallas_call(
        _out_mlp_kernel,
        grid=(m // tm,),
        in_specs=[row(D_MODEL), row(o.shape[1]), row(MEM_WIDTH), full(wo), full(wm), full(g), full(wu), full(wd)],
        out_specs=row(D_MODEL),
        out_shape=jax.ShapeDtypeStruct((m, D_MODEL), F32),
        compiler_params=_cparams(("parallel",)),
    )(x, o, xo, wo, wm, g, wu, wd)


def _pad_heads(w, heads, d, dpad):
    lead = w.shape[:-1]
    w = w.reshape(lead + (heads, d))
    w = jnp.pad(w, [(0, 0)] * len(lead) + [(0, 0), (0, dpad - d)])
    return w.reshape(lead + (heads * dpad,))


def _gla_weights(w_in, w_a2, b_a, b_r, o_gain, w_out):
    hk = GLA_HEADS * GLA_DK
    splits = np.cumsum((hk, hk, TOK_WIDTH, GLA_GATE_RANK, TOK_WIDTH))
    wq, wk, wv, wa, wr, wx = jnp.split(w_in, [int(s) for s in splits], axis=1)
    w = jnp.concatenate([
        _pad_heads(wq, GLA_HEADS, GLA_DK, GLA_DK_PAD), _pad_heads(wk, GLA_HEADS, GLA_DK, GLA_DK_PAD),
        _pad_heads(wv, GLA_HEADS, GLA_DV, GLA_DV_PAD), _pad_heads(wr, GLA_HEADS, GLA_DV, GLA_DV_PAD),
        wx, jnp.pad(wa, ((0, 0), (0, LANES - GLA_GATE_RANK)))], axis=1).astype(BF16)
    wa2 = jnp.pad(_pad_heads(w_a2, GLA_HEADS, GLA_DK, GLA_DK_PAD), ((0, LANES - GLA_GATE_RANK), (0, 0))).astype(BF16)
    ba = _pad_heads(b_a[None, :], GLA_HEADS, GLA_DK, GLA_DK_PAD)
    br = _pad_heads(b_r[None, :], GLA_HEADS, GLA_DV, GLA_DV_PAD)
    og = jnp.pad(o_gain[None, :], ((0, 0), (0, GLA_DV_PAD - GLA_DV)))
    wo = _pad_heads(w_out[:TOK_WIDTH].T, GLA_HEADS, GLA_DV, GLA_DV_PAD).T.astype(BF16)
    wm = w_out[TOK_WIDTH:].astype(BF16)
    return w, wa2, ba, br, og, wo, wm


def _fox_weights(w_in, b_f, q_gain, k_gain, w_out):
    splits = np.cumsum((TOK_WIDTH, TOK_WIDTH, TOK_WIDTH, FOX_HEADS))
    wq, wk, wv, wf, wx = jnp.split(w_in, [int(s) for s in splits], axis=1)
    w = jnp.concatenate([wq, wk, wv, wx, jnp.pad(wf, ((0, 0), (0, LANES - FOX_HEADS)))], axis=1).astype(BF16)
    bfp = jnp.pad(b_f[None, :], ((0, 0), (0, LANES - FOX_HEADS)))
    qg = jnp.tile(q_gain, FOX_HEADS)[None, :]
    kg = jnp.tile(k_gain, FOX_HEADS)[None, :]
    return w, bfp, qg, kg, w_out[:TOK_WIDTH].astype(BF16), w_out[TOK_WIDTH:].astype(BF16)


TM = 512
GLA_CHUNK_TOKENS = 64
GLA_BLOCK_TOKENS = 512
FOX_TQ = 512
FOX_TK = 1024
CUMSUM_ROWS = 512


def kernel(x_prompt, x_sample, mem_prompt, cache_fox_k, cache_fox_v, cache_fox_logf, page_table, state_gla, cache_mem_k, cache_mem_v, norm_mix, norm_mlp, norm_mem, gla_w_in, gla_w_a2, gla_b_a, gla_b_r, gla_o_norm, gla_w_out, fox_w_in, fox_b_f, fox_q_norm, fox_k_norm, fox_w_out, mem_w_kv, mem_q_norm, mem_k_norm, mlp_w_up, mlp_w_down):
    nb, seq, _ = x_prompt.shape
    ns, tdec, _ = x_sample.shape
    n_pool, n_fox = cache_fox_k.shape[0], cache_fox_k.shape[1]
    mp, ms = nb * seq, ns * tdec
    yp = x_prompt.reshape(mp, D_MODEL)
    ys = x_sample.reshape(ms, D_MODEL)
    mem2 = mem_prompt.reshape(nb * N_MEM, D_MODEL)
    ck = cache_fox_k.reshape(n_pool, n_fox, PAGE_SIZE, TOK_WIDTH)
    cv = cache_fox_v.reshape(n_pool, n_fox, PAGE_SIZE, TOK_WIDTH)
    clt = jnp.pad(jnp.swapaxes(cache_fox_logf, 2, 3), ((0, 0), (0, 0), (0, 2 * SUBLANES - FOX_HEADS), (0, 0)))
    cmk = cache_mem_k.reshape(DEPTH, ns, N_MEM, MEM_WIDTH)
    cmv = cache_mem_v.reshape(DEPTH, ns, N_MEM, MEM_WIDTH)
    tms = min(TM, ms)

    fk_p, fv_p, fl_p, gs_p, mk_all, mv_all = [], [], [], [], [], []
    fk_s, fv_s, fl_s, gs_s = [], [], [], []
    for l in range(DEPTH):
        i = l // 2
        g_mix = norm_mix[l][None, :]
        mqg = jnp.tile(mem_q_norm[l], MEM_HEADS)[None, :]
        mkg = jnp.tile(mem_k_norm[l], MEM_HEADS)[None, :]
        mk_p, mv_p = _mem_kv(mem2, norm_mem[l][None, :], mem_w_kv[l].astype(BF16), mkg)
        mk_all.append(mk_p.reshape(nb, N_MEM, MEM_HEADS, MEM_HEAD_DIM))
        mv_all.append(mv_p.reshape(nb, N_MEM, MEM_HEADS, MEM_HEAD_DIM))
        mk_p = mk_p.reshape(nb, N_MEM, MEM_WIDTH)
        mv_p = mv_p.reshape(nb, N_MEM, MEM_WIDTH)
        if l % 2 == 0:
            w, wa2, ba, br, og, wo, wm = _gla_weights(gla_w_in[i], gla_w_a2[i], gla_b_a[i], gla_b_r[i],
                                                      gla_o_norm[i], gla_w_out[i])
            outs = []
            for x2, bsz, t, tm, s0 in ((yp, nb, seq, TM, None), (ys, ns, tdec, tms, state_gla[i])):
                q, k, v, r, xq, la = _gla_proj(x2, g_mix, w, wa2, ba, tm)
                r3 = lambda a: a.reshape(bsz, t, a.shape[-1])
                o, s_new = _gla_mix(r3(q), r3(k), r3(la), r3(v), r3(r), og, br, s0,
                                    chunk=GLA_CHUNK_TOKENS if t >= GLA_CHUNK_TOKENS else 2 * SUBLANES,
                                    tblk=GLA_BLOCK_TOKENS)
                outs.append((o.reshape(bsz * t, -1), r3(xq), s_new))
            (o_p, xq_p, sp), (o_s, xq_s, ss) = outs
            gs_p.append(sp)
            gs_s.append(ss)
        else:
            w, bfp, qg, kg, wo, wm = _fox_weights(fox_w_in[i], fox_b_f[i], fox_q_norm[i], fox_k_norm[i], fox_w_out[i])
            qb, kf, kb, vf, vb, xq_p, lf = _fox_proj(yp, g_mix, w, qg, kg, bfp, TM)
            r3 = lambda a: a.reshape(nb, seq, a.shape[-1])
            c = _cumsum_rows(r3(lf), CUMSUM_ROWS)
            ct = jnp.swapaxes(c[:, :, :2 * SUBLANES], 1, 2)
            o_p = _fox_flash(r3(qb), r3(kb), r3(vb), c, ct, min(FOX_TQ, seq), min(FOX_TK, seq)).reshape(mp, TOK_WIDTH)
            xq_p = r3(xq_p)
            fk_p.append(kf.reshape(nb, seq, FOX_HEADS, FOX_HEAD_DIM))
            fv_p.append(vf.reshape(nb, seq, FOX_HEADS, FOX_HEAD_DIM))
            fl_p.append(lf[:, :FOX_HEADS].reshape(nb, seq, FOX_HEADS))
            qb, kf, kb, vf, vb, xq_s, lf = _fox_proj(ys, g_mix, w, qg, kg, bfp, tms)
            r3 = lambda a: a.reshape(ns, tdec, a.shape[-1])
            o_s = _fox_decode(r3(qb.astype(F32)), r3(kf), r3(vf), r3(lf), ck, cv, clt, page_table,
                              i).reshape(ms, TOK_WIDTH)
            xq_s = r3(xq_s)
            fk_s.append(kf.reshape(ns, tdec, FOX_HEADS, FOX_HEAD_DIM))
            fv_s.append(vf.reshape(ns, tdec, FOX_HEADS, FOX_HEAD_DIM))
            fl_s.append(lf[:, :FOX_HEADS].reshape(ns, tdec, FOX_HEADS))
        xo_p = _mem_attn_prompt(xq_p, mk_p, mv_p, mqg, TM).reshape(mp, MEM_WIDTH)
        xo_s = _mem_attn_decode(xq_s, cmk, cmv, mqg, l).reshape(ms, MEM_WIDTH)
        g_mlp = norm_mlp[l][None, :]
        wu = mlp_w_up[l].astype(BF16)
        wd = mlp_w_down[l].astype(BF16)
        yp = _out_mlp(yp, o_p, xo_p, wo, wm, g_mlp, wu, wd, TM)
        ys = _out_mlp(ys, o_s, xo_s, wo, wm, g_mlp, wu, wd, tms)
    return (yp.reshape(nb, seq, D_MODEL), ys.reshape(ns, tdec, D_MODEL),
            jnp.stack(fk_p, axis=1), jnp.stack(fv_p, axis=1), jnp.stack(fl_p, axis=1),
            jnp.stack(gs_p, axis=0), jnp.stack(mk_all, axis=0), jnp.stack(mv_all, axis=0),
            jnp.stack(fk_s, axis=1), jnp.stack(fv_s, axis=1), jnp.stack(fl_s, axis=1),
            jnp.stack(gs_s, axis=0))
```

```python
import functools
import math

import numpy as np
import jax
import jax.numpy as jnp
from jax import lax
from jax.experimental import pallas as pl
from jax.experimental.pallas import tpu as pltpu

F32 = jnp.float32
BF16 = jnp.bfloat16

D_MODEL = 1024
DEPTH = 4
PAGE_SIZE = 128
TOK_WIDTH = 768
MEM_WIDTH = 256
N_MEM = 256
MEM_HEADS = 4
MEM_HEAD_DIM = 64
GLA_HEADS = 4
GLA_DV = 192
GLA_DK = 96
GLA_GATE_RANK = 16
GLA_TAU = 16.0
FOX_HEAD_DIM = 64
FOX_HEADS = 12
D_FF = 4 * D_MODEL
EPS = 1e-6

LANES = 128
SUBLANES = 8
GLA_DK_PAD = 128
GLA_DV_PAD = 256
NEG_BIG = -1e30
VMEM_LIMIT = 56 * 1024 * 1024


def _cparams(sem):
    return pltpu.CompilerParams(dimension_semantics=sem, vmem_limit_bytes=VMEM_LIMIT)


def _resident(a):
    return pl.BlockSpec(a.shape, lambda i: (0,) * a.ndim, pipeline_mode=pl.Buffered(1))


def _dot(a, b):
    return jnp.dot(a, b, preferred_element_type=F32)


def _dot_nt(a, b):
    return lax.dot_general(a, b, (((1,), (1,)), ((), ())), preferred_element_type=F32)


def _dot_tn(a, b):
    return lax.dot_general(a, b, (((0,), (0,)), ((), ())), preferred_element_type=F32)


def _split3(x):
    hi = x.astype(BF16)
    r1 = x - hi.astype(F32)
    mid = r1.astype(BF16)
    lo = (r1 - mid.astype(F32)).astype(BF16)
    return hi, mid, lo


def _dot01(m01, x):
    hi, mid, lo = _split3(x)
    return (_dot(m01, lo) + _dot(m01, mid)) + _dot(m01, hi)


def _rms(x, g):
    return x * lax.rsqrt(jnp.mean(x * x, axis=-1, keepdims=True) + EPS) * g


def _log_sigmoid(x):
    return jnp.minimum(x, 0.0) - jnp.log(1.0 + jnp.exp(-jnp.abs(x)))


def _half_mask(shape):
    lane = lax.broadcasted_iota(jnp.int32, shape, len(shape) - 1)
    return (lane & FOX_HEAD_DIM) == 0


def _log2(n):
    k = int(n).bit_length() - 1
    assert 1 << k == n, n
    return k


def _ones_where(cond):
    return jnp.where(cond, 1.0, 0.0).astype(BF16)


def _head_rms_pairs(x, gain):
    n = x.shape[-1] // LANES
    outs = []
    for c in range(n):
        xc = x[:, c * LANES:(c + 1) * LANES]
        sq = xc * xc
        even = _half_mask(xc.shape)
        s_e = jnp.sum(jnp.where(even, sq, 0.0), axis=-1, keepdims=True)
        s_o = jnp.sum(jnp.where(even, 0.0, sq), axis=-1, keepdims=True)
        rs = jnp.where(even, lax.rsqrt(s_e / FOX_HEAD_DIM + EPS), lax.rsqrt(s_o / FOX_HEAD_DIM + EPS))
        outs.append(xc * rs)
    return jnp.concatenate(outs, axis=-1) * gain


GLA_COLS = (GLA_HEADS * GLA_DK_PAD, GLA_HEADS * GLA_DK_PAD, GLA_HEADS * GLA_DV_PAD,
            GLA_HEADS * GLA_DV_PAD, MEM_WIDTH, LANES)
GLA_OFF = tuple(int(v) for v in np.cumsum((0,) + GLA_COLS))


def _gla_proj_kernel(x_ref, g_ref, w_ref, wa_ref, ba_ref, q_ref, k_ref, v_ref, r_ref, xq_ref, la_ref):
    xn = _rms(x_ref[...], g_ref[...]).astype(BF16)
    y = _dot(xn, w_ref[...])
    o = GLA_OFF
    q_ref[...] = y[:, o[0]:o[1]] * (GLA_DK ** -0.5)
    k_ref[...] = y[:, o[1]:o[2]]
    v_ref[...] = y[:, o[2]:o[3]].astype(v_ref.dtype)
    r_ref[...] = y[:, o[3]:o[4]]
    xq_ref[...] = y[:, o[4]:o[5]]
    a = _dot(y[:, o[5]:o[6]].astype(BF16), wa_ref[...]) + ba_ref[...]
    la_ref[...] = _log_sigmoid(a) * (1.0 / GLA_TAU)


def _gla_proj(x, g, w, wa, ba, tm):
    m = x.shape[0]
    row = lambda n: pl.BlockSpec((tm, n), lambda i: (i, 0))
    full = _resident
    hk, hv = GLA_HEADS * GLA_DK_PAD, GLA_HEADS * GLA_DV_PAD
    out_shape = (jax.ShapeDtypeStruct((m, hk), F32), jax.ShapeDtypeStruct((m, hk), F32),
                 jax.ShapeDtypeStruct((m, hv), F32), jax.ShapeDtypeStruct((m, hv), F32),
                 jax.ShapeDtypeStruct((m, MEM_WIDTH), F32), jax.ShapeDtypeStruct((m, hk), F32))
    return pl.pallas_call(
        _gla_proj_kernel,
        grid=(m // tm,),
        in_specs=[row(D_MODEL), full(g), full(w), full(wa), full(ba)],
        out_specs=(row(hk), row(hk), row(hv), row(hv), row(MEM_WIDTH), row(hk)),
        out_shape=out_shape,
        compiler_params=_cparams(("parallel",)),
    )(x, g, w, wa, ba)


FOX_COLS = (TOK_WIDTH, TOK_WIDTH, TOK_WIDTH, MEM_WIDTH, LANES)
FOX_OFF = tuple(int(v) for v in np.cumsum((0,) + FOX_COLS))


def _fox_proj_kernel(x_ref, g_ref, w_ref, qg_ref, kg_ref, bf_ref,
                     qb_ref, kf_ref, kb_ref, vf_ref, vb_ref, xq_ref, lf_ref):
    xn = _rms(x_ref[...], g_ref[...]).astype(BF16)
    y = _dot(xn, w_ref[...])
    o = FOX_OFF
    q = _head_rms_pairs(y[:, o[0]:o[1]], qg_ref[...])
    qb_ref[...] = (q * (FOX_HEAD_DIM ** -0.5)).astype(BF16)
    k = _head_rms_pairs(y[:, o[1]:o[2]], kg_ref[...])
    kf_ref[...] = k
    kb_ref[...] = k.astype(BF16)
    v = y[:, o[2]:o[3]]
    vf_ref[...] = v
    vb_ref[...] = v.astype(BF16)
    xq_ref[...] = y[:, o[3]:o[4]]
    lf_ref[...] = _log_sigmoid(y[:, o[4]:o[5]] + bf_ref[...])


def _fox_proj(x, g, w, qg, kg, bfp, tm):
    m = x.shape[0]
    row = lambda n: pl.BlockSpec((tm, n), lambda i: (i, 0))
    full = _resident
    tw = TOK_WIDTH
    out_shape = (jax.ShapeDtypeStruct((m, tw), BF16), jax.ShapeDtypeStruct((m, tw), F32),
                 jax.ShapeDtypeStruct((m, tw), BF16), jax.ShapeDtypeStruct((m, tw), F32),
                 jax.ShapeDtypeStruct((m, tw), BF16), jax.ShapeDtypeStruct((m, MEM_WIDTH), F32),
                 jax.ShapeDtypeStruct((m, LANES), F32))
    return pl.pallas_call(
        _fox_proj_kernel,
        grid=(m // tm,),
        in_specs=[row(D_MODEL), full(g), full(w), full(qg), full(kg), full(bfp)],
        out_specs=(row(tw), row(tw), row(tw), row(tw), row(tw), row(MEM_WIDTH), row(LANES)),
        out_shape=out_shape,
        compiler_params=_cparams(("parallel",)),
    )(x, g, w, qg, kg, bfp)


def _mem_kv_kernel(x_ref, g_ref, w_ref, kg_ref, k_ref, v_ref):
    xn = _rms(x_ref[...], g_ref[...]).astype(BF16)
    y = _dot(xn, w_ref[...])
    k_ref[...] = _head_rms_pairs(y[:, :MEM_WIDTH], kg_ref[...])
    v_ref[...] = y[:, MEM_WIDTH:]


def _mem_kv(x, g, w, kg):
    m = x.shape[0]
    full = _resident
    out = jax.ShapeDtypeStruct((m, MEM_WIDTH), F32)
    return pl.pallas_call(
        _mem_kv_kernel,
        grid=(1,),
        in_specs=[full(x), full(g), full(w), full(kg)],
        out_specs=(pl.BlockSpec((m, MEM_WIDTH), lambda i: (0, 0)),) * 2,
        out_shape=(out, out),
        compiler_params=_cparams(("arbitrary",)),
    )(x, g, w, kg)


def _gla_levels(c):
    out, h = [], c // 2
    while h >= SUBLANES:
        out.append(h)
        h //= 2
    return out


def _gla_cum_matrix(c):
    t = np.arange(c)
    blocks = [(t[None, :] <= t[:, None])]
    for h in _gla_levels(c):
        anchor = (t // (2 * h)) * (2 * h) + h - 1
        blocks.append(t[None, :] <= anchor[:, None])
    return jnp.asarray(np.concatenate(blocks, axis=0).astype(np.float32), dtype=BF16)


def _gla_kernel(*refs, chunk, n_sub, t_valid, has_s0, nbatch):
    if has_s0:
        cum_ref, q_ref, k_ref, la_ref, v_ref, r_ref, og_ref, br_ref, s0_ref, o_ref, sout_ref, st_ref, pad_ref = refs
    else:
        cum_ref, q_ref, k_ref, la_ref, v_ref, r_ref, og_ref, br_ref, o_ref, sout_ref, st_ref, pad_ref = refs
        s0_ref = None
    c = chunk
    levels = _gla_levels(c)
    nb = c // SUBLANES
    step = pl.program_id(1)
    chains = [(bi, h) for bi in range(nbatch) for h in range(GLA_HEADS)]

    @pl.when(step == 0)
    def _():
        for ci, (bi, h) in enumerate(chains):
            if has_s0:
                pad_ref[...] = jnp.zeros(pad_ref.shape, F32)
                pad_ref[:GLA_DK, :GLA_DV] = s0_ref[bi, h]
                st_ref[ci] = pad_ref[...].T
            else:
                st_ref[ci] = jnp.zeros(st_ref.shape[1:], F32)

    def load(ref, bi, rows, h, width):
        x = ref[bi, rows, h * width:(h + 1) * width]
        if t_valid < c:
            x = jnp.concatenate([x, jnp.zeros((c - t_valid, width), F32)], axis=0)
        return x

    def do_chunk(ci, bi, h, rows):
        q = load(q_ref, bi, rows, h, GLA_DK_PAD)
        k = load(k_ref, bi, rows, h, GLA_DK_PAD)
        la = load(la_ref, bi, rows, h, GLA_DK_PAD)
        v = load(v_ref, bi, rows, h, GLA_DV_PAD)
        vb = v.astype(BF16)
        cum = _dot01(cum_ref[...], la)
        b = cum[:c]
        b_last = b[c - 1:c, :]
        st = st_ref[ci]
        o = _dot_nt((q * jnp.exp(b)).astype(BF16), st.astype(BF16))
        if levels:
            ti = lax.broadcasted_iota(jnp.int32, (c, c), 0)
            si = lax.broadcasted_iota(jnp.int32, (c, c), 1)
            row = lax.broadcasted_iota(jnp.int32, (c, GLA_DK_PAD), 0)
            attn = jnp.zeros((c, c), F32)
            for li, half in enumerate(levels):
                anchor = cum[(li + 1) * c:(li + 2) * c]
                gdec = jnp.exp(-jnp.abs(b - anchor))
                first = (row & half) == 0
                ql = jnp.where(first, 0.0, q * gdec).astype(BF16)
                kl = jnp.where(first, k * gdec, 0.0).astype(BF16)
                a = _dot_nt(ql, kl)
                sh = _log2(2 * half)
                attn = attn + jnp.where((ti >> sh) == (si >> sh), a, 0.0)
            o = o + _dot(attn.astype(BF16), vb)
        q3 = q.reshape(nb, SUBLANES, GLA_DK_PAD)
        k3 = k.reshape(nb, SUBLANES, GLA_DK_PAD)
        b3 = b.reshape(nb, SUBLANES, GLA_DK_PAD)
        v3 = v.reshape(nb, SUBLANES, GLA_DV_PAD)
        sub = lax.broadcasted_iota(jnp.int32, (nb, SUBLANES, GLA_DK_PAD), 1)
        od = jnp.zeros((nb, SUBLANES, GLA_DV_PAD), F32)
        for j in range(SUBLANES):
            kj = jnp.broadcast_to(k3[:, j:j + 1, :], k3.shape)
            bj = jnp.broadcast_to(b3[:, j:j + 1, :], b3.shape)
            e = jnp.exp(jnp.where(sub >= j, b3 - bj, -jnp.inf))
            col = jnp.sum(q3 * kj * e, axis=-1, keepdims=True)
            od = od + col * jnp.broadcast_to(v3[:, j:j + 1, :], v3.shape)
        o = o + od.reshape(c, GLA_DV_PAD)
        kg = (k * jnp.exp(b_last - b)).astype(BF16)
        st_ref[ci] = st * jnp.exp(b_last) + _dot_tn(vb, kg)
        ms = jnp.sum(o * o, axis=-1, keepdims=True) * (1.0 / GLA_DV)
        on = o * lax.rsqrt(ms + EPS) * og_ref[...]
        r = load(r_ref, bi, rows, h, GLA_DV_PAD) + br_ref[:, h * GLA_DV_PAD:(h + 1) * GLA_DV_PAD]
        res = on * (r * (1.0 / (1.0 + jnp.exp(-r))))
        return res[:t_valid] if t_valid < c else res

    def all_chains(rows):
        for ci, (bi, h) in enumerate(chains):
            res = do_chunk(ci, bi, h, rows)
            o_ref[bi, rows, h * GLA_DV_PAD:(h + 1) * GLA_DV_PAD] = res.astype(o_ref.dtype)

    if n_sub == 1:
        all_chains(slice(None))
    else:
        def body(i, carry):
            all_chains(pl.ds(pl.multiple_of(i * c, c), c))
            return carry
        lax.fori_loop(0, n_sub, body, 0)

    @pl.when(step == pl.num_programs(1) - 1)
    def _():
        for ci, (bi, h) in enumerate(chains):
            pad_ref[...] = st_ref[ci].T
            sout_ref[bi, h] = pad_ref[:GLA_DK, :GLA_DV]


def _gla_mix(q, k, la, v, r, og, br, s0, *, chunk, tblk, nbatch):
    bsz, t, _ = q.shape
    has_s0 = s0 is not None
    if t < chunk:
        t_valid, n_sub, tblk = t, 1, t
    else:
        t_valid, n_sub = chunk, tblk // chunk
    cum = _gla_cum_matrix(chunk)
    hk, hv = GLA_HEADS * GLA_DK_PAD, GLA_HEADS * GLA_DV_PAD
    kblk = pl.BlockSpec((nbatch, tblk, hk), lambda b, i: (b, i, 0))
    vblk = pl.BlockSpec((nbatch, tblk, hv), lambda b, i: (b, i, 0))
    sblk = pl.BlockSpec((nbatch, GLA_HEADS, GLA_DK, GLA_DV), lambda b, i: (b, 0, 0, 0))
    in_specs = [pl.BlockSpec(cum.shape, lambda b, i: (0, 0)), kblk, kblk, kblk, vblk, vblk,
                pl.BlockSpec((1, GLA_DV_PAD), lambda b, i: (0, 0)),
                pl.BlockSpec((1, hv), lambda b, i: (0, 0))]
    args = [cum, q, k, la, v, r, og, br]
    if has_s0:
        in_specs.append(sblk)
        args.append(s0)
    kern = functools.partial(_gla_kernel, chunk=chunk, n_sub=n_sub, t_valid=t_valid, has_s0=has_s0, nbatch=nbatch)
    return pl.pallas_call(
        kern,
        grid=(bsz // nbatch, t // tblk),
        in_specs=in_specs,
        out_specs=(vblk, sblk),
        out_shape=(jax.ShapeDtypeStruct((bsz, t, hv), BF16 if t >= chunk else F32),
                   jax.ShapeDtypeStruct((bsz, GLA_HEADS, GLA_DK, GLA_DV), F32)),
        scratch_shapes=[pltpu.VMEM((nbatch * GLA_HEADS, GLA_DV_PAD, GLA_DK_PAD), F32),
                        pltpu.VMEM((GLA_DK_PAD, GLA_DV_PAD), F32)],
        compiler_params=_cparams(("parallel", "arbitrary")),
    )(*args)


def _cumsum_kernel(tri_ref, x_ref, c_ref, carry_ref):
    @pl.when(pl.program_id(1) == 0)
    def _():
        carry_ref[...] = jnp.zeros(carry_ref.shape, F32)
    c = _dot01(tri_ref[...], x_ref[0]) + carry_ref[...]
    c_ref[0] = c
    carry_ref[...] = c[c.shape[0] - 1:, :]


def _cumsum_rows(x, tc):
    bsz, t, n = x.shape
    tri = jnp.asarray(np.tril(np.ones((tc, tc), np.float32)), dtype=BF16)
    blk = pl.BlockSpec((1, tc, n), lambda b, i: (b, i, 0))
    return pl.pallas_call(
        _cumsum_kernel,
        grid=(bsz, t // tc),
        in_specs=[pl.BlockSpec((tc, tc), lambda b, i: (0, 0)), blk],
        out_specs=blk,
        out_shape=jax.ShapeDtypeStruct((bsz, t, n), F32),
        scratch_shapes=[pltpu.VMEM((1, n), F32)],
        compiler_params=_cparams(("parallel", "arbitrary")),
    )(tri, x)


def _fox_flash_kernel(qi_ref, kj_ref, q_ref, k_ref, v_ref, c_ref, ct_ref, o_ref, m_ref, l_ref, cq_ref, acc_ref,
                      *, tq, tk):
    pair = pl.program_id(1)
    qi = qi_ref[pl.program_id(2)]
    kj = kj_ref[pl.program_id(2)]
    last = ((qi + 1) * tq - 1) // tk

    @pl.when(kj == 0)
    def _():
        m_ref[...] = jnp.full(m_ref.shape, NEG_BIG, F32)
        l_ref[...] = jnp.zeros(l_ref.shape, F32)
        acc_ref[...] = jnp.zeros(acc_ref.shape, F32)
        cb = c_ref[0]
        lane = lax.broadcasted_iota(jnp.int32, cb.shape, 1)
        for hh in range(2):
            cq_ref[hh] = jnp.sum(jnp.where(lane == 2 * pair + hh, cb, 0.0), axis=-1, keepdims=True)

    def update(masked):
        q = q_ref[0]
        k = k_ref[0]
        v = v_ref[0]
        even = _half_mask(q.shape)
        zero = jnp.zeros(q.shape, q.dtype)
        qs = (jnp.where(even, q, zero), jnp.where(even, zero, q))
        if masked:
            rowi = qi * tq + lax.broadcasted_iota(jnp.int32, (tq, tk), 0)
            coli = kj * tk + lax.broadcasted_iota(jnp.int32, (tq, tk), 1)
            keep = rowi >= coli
        alphas, pvs = [], []
        for hh in range(2):
            s = _dot_nt(qs[hh], k) + cq_ref[hh] - ct_ref[0, pl.ds(2 * pair + hh, 1), :]
            if masked:
                s = jnp.where(keep, s, NEG_BIG)
            m_old = m_ref[hh]
            m_new = jnp.maximum(m_old, jnp.max(s, axis=-1, keepdims=True))
            alpha = jnp.exp(m_old - m_new)
            p = jnp.exp(s - m_new)
            l_ref[hh] = alpha * l_ref[hh] + jnp.sum(p, axis=-1, keepdims=True)
            m_ref[hh] = m_new
            alphas.append(alpha)
            pvs.append(_dot(p.astype(BF16), v))
        even_o = _half_mask(acc_ref.shape)
        acc_ref[...] = (jnp.where(even_o, alphas[0], alphas[1]) * acc_ref[...]
                        + jnp.where(even_o, pvs[0], pvs[1]))

    crosses = (kj + 1) * tk - 1 > qi * tq

    @pl.when(crosses)
    def _():
        update(True)

    @pl.when(jnp.logical_not(crosses))
    def _():
        update(False)

    @pl.when(kj == last)
    def _():
        even_o = _half_mask(acc_ref.shape)
        inv = jnp.where(even_o, 1.0 / l_ref[0], 1.0 / l_ref[1])
        o_ref[0] = (acc_ref[...] * inv).astype(o_ref.dtype)


def _fox_flash(q, k, v, c, ct, tq, tk):
    bsz, t, _ = q.shape
    npair = FOX_HEADS // 2
    pairs = [(i, j) for i in range(t // tq) for j in range(((i + 1) * tq - 1) // tk + 1)]
    qi_tbl = jnp.asarray([i for i, _ in pairs], jnp.int32)
    kj_tbl = jnp.asarray([j for _, j in pairs], jnp.int32)
    qblk = pl.BlockSpec((1, tq, LANES), lambda b, p, s, qi, kj: (b, qi[s], p))
    kblk = pl.BlockSpec((1, tk, LANES), lambda b, p, s, qi, kj: (b, kj[s], p))
    grid_spec = pltpu.PrefetchScalarGridSpec(
        num_scalar_prefetch=2,
        grid=(bsz, npair, len(pairs)),
        in_specs=[qblk, kblk, kblk,
                  pl.BlockSpec((1, tq, LANES), lambda b, p, s, qi, kj: (b, qi[s], 0)),
                  pl.BlockSpec((1, ct.shape[1], tk), lambda b, p, s, qi, kj: (b, 0, kj[s]))],
        out_specs=qblk,
        scratch_shapes=[pltpu.VMEM((2, tq, 1), F32), pltpu.VMEM((2, tq, 1), F32),
                        pltpu.VMEM((2, tq, 1), F32), pltpu.VMEM((tq, LANES), F32)],
    )
    return pl.pallas_call(
        functools.partial(_fox_flash_kernel, tq=tq, tk=tk),
        grid_spec=grid_spec,
        out_shape=jax.ShapeDtypeStruct((bsz, t, TOK_WIDTH), BF16),
        compiler_params=_cparams(("parallel", "parallel", "arbitrary")),
    )(qi_tbl, kj_tbl, q, k, v, c, ct)


def _head_rows(x, heads):
    lane = lax.broadcasted_iota(jnp.int32, x.shape, 1) >> _log2(FOX_HEAD_DIM)
    zero = jnp.zeros(x.shape, x.dtype)
    return jnp.concatenate([jnp.where(lane == h, x, zero) for h in range(heads)], axis=0)


def _head_diag(z, heads, t):
    lane = lax.broadcasted_iota(jnp.int32, (t, z.shape[1]), 1) >> _log2(FOX_HEAD_DIM)
    out = jnp.zeros((t, z.shape[1]), z.dtype)
    for h in range(heads):
        out = jnp.where(lane == h, z[h * t:(h + 1) * t, :], out)
    return out


def _pad_rows(x, n):
    return jnp.concatenate([x, jnp.zeros((n - x.shape[0], x.shape[1]), x.dtype)], axis=0)


def _fox_decode_kernel(pt_ref, q_ref, kn_ref, vn_ref, lfn_ref, *rest, t_new, pps):
    del pt_ref
    kc_refs, vc_refs, lc_refs = rest[:pps], rest[pps:2 * pps], rest[2 * pps:3 * pps]
    o_ref, wq_ref, m_ref, l_ref, acc_ref, carry_ref, cnk_ref = rest[3 * pps:]
    p = pl.program_id(1)
    npg = pl.num_programs(1)
    heads = FOX_HEADS
    rows = heads * t_new
    tpad = 2 * SUBLANES

    @pl.when(p == 0)
    def _():
        wq_ref[...] = _head_rows(q_ref[0], heads).astype(BF16)
        m_ref[...] = jnp.full(m_ref.shape, NEG_BIG, F32)
        l_ref[...] = jnp.zeros(l_ref.shape, F32)
        acc_ref[...] = jnp.zeros(acc_ref.shape, F32)
        carry_ref[...] = jnp.zeros(carry_ref.shape, F32)
        lfn = jnp.concatenate([lfn_ref[0], jnp.zeros((tpad - t_new, LANES), F32)], axis=0)
        ti = lax.broadcasted_iota(jnp.int32, (tpad, tpad), 0)
        si = lax.broadcasted_iota(jnp.int32, (tpad, tpad), 1)
        cs = _dot01(_ones_where(si <= ti), lfn)
        rh = lax.broadcasted_iota(jnp.int32, (rows, LANES), 0) >> _log2(t_new)
        lh = lax.broadcasted_iota(jnp.int32, (rows, LANES), 1)
        hsel = _ones_where(rh == lh)
        hi, mid, lo = _split3(cs)
        cnk_ref[...] = (_dot_nt(hsel, lo) + _dot_nt(hsel, mid)) + _dot_nt(hsel, hi)

    def online(s, pv_of):
        m_old = m_ref[...]
        m_new = jnp.maximum(m_old, jnp.max(s, axis=-1, keepdims=True))
        alpha = jnp.exp(m_old - m_new)
        pr = jnp.exp(s - m_new)
        l_ref[...] = alpha * l_ref[...] + jnp.sum(pr, axis=-1, keepdims=True)
        m_ref[...] = m_new
        acc_ref[...] = alpha * acc_ref[...] + pv_of(pr.astype(BF16))

    cnk = cnk_ref[...]
    ri = lax.broadcasted_iota(jnp.int32, cnk.shape, 0) & (t_new - 1)
    ci = lax.broadcasted_iota(jnp.int32, cnk.shape, 1)
    cn_col = jnp.sum(jnp.where(ri == ci, cnk, 0.0), axis=-1, keepdims=True)

    ji = lax.broadcasted_iota(jnp.int32, (PAGE_SIZE, PAGE_SIZE), 0)
    si2 = lax.broadcasted_iota(jnp.int32, (PAGE_SIZE, PAGE_SIZE), 1)
    after = _ones_where(ji > si2)
    for j in range(pps):
        lp = lc_refs[j][0, 0]
        suffix = _dot01_rhs(lp, after) + carry_ref[...]
        carry_ref[...] = carry_ref[...] + jnp.sum(lp, axis=-1, keepdims=True)
        r96 = jnp.concatenate([jnp.broadcast_to(suffix[h:h + 1, :], (t_new, PAGE_SIZE)) for h in range(heads)],
                              axis=0)
        kt = kc_refs[j][0, 0].reshape(TOK_WIDTH, PAGE_SIZE).astype(BF16)
        vt = vc_refs[j][0, 0].reshape(TOK_WIDTH, PAGE_SIZE).astype(BF16)
        online(_dot(wq_ref[...], kt) + r96 + cn_col, lambda pr, vt=vt: _dot_nt(pr, vt))

    @pl.when(p == npg - 1)
    def _():
        kn = _pad_rows(kn_ref[0], tpad).astype(BF16)
        vn = _pad_rows(vn_ref[0], tpad).astype(BF16)
        sn = jnp.where(ci <= ri, _dot_nt(wq_ref[...], kn) + cn_col - cnk, NEG_BIG)
        online(sn, lambda pr: _dot(pr, vn))
        o_ref[0] = _head_diag(acc_ref[...] / l_ref[...], heads, t_new).astype(o_ref.dtype)


def _dot01_rhs(x, m01):
    hi, mid, lo = _split3(x)
    return (_dot(lo, m01) + _dot(mid, m01)) + _dot(hi, m01)


def _fox_decode(q, kn, vn, lfn, cache_k, cache_v, cache_lt, page_table, layer, pps):
    nseq, t_new, _ = q.shape
    heads, hd = cache_k.shape[2], cache_k.shape[3]
    npg = page_table.shape[1]
    pps = pps if npg % pps == 0 else 1
    rows = heads * t_new
    tpad = 2 * SUBLANES
    seq = lambda n: pl.BlockSpec((1, t_new, n), lambda b, p, pt: (b, 0, 0))

    def page_idx(j, ndim):
        return lambda b, p, pt: (pt[b, npg - 1 - (p * pps + j)], layer) + (0,) * (ndim - 2)

    kv_specs = [pl.BlockSpec((1, 1, heads, hd, PAGE_SIZE), page_idx(j, 5)) for j in range(pps)]
    lt_specs = [pl.BlockSpec((1, 1, tpad, PAGE_SIZE), page_idx(j, 4)) for j in range(pps)]
    grid_spec = pltpu.PrefetchScalarGridSpec(
        num_scalar_prefetch=1,
        grid=(nseq, npg // pps),
        in_specs=[seq(TOK_WIDTH), seq(TOK_WIDTH), seq(TOK_WIDTH), seq(LANES)] + kv_specs + kv_specs + lt_specs,
        out_specs=seq(TOK_WIDTH),
        scratch_shapes=[pltpu.VMEM((rows, TOK_WIDTH), BF16), pltpu.VMEM((rows, 1), F32),
                        pltpu.VMEM((rows, 1), F32), pltpu.VMEM((rows, TOK_WIDTH), F32),
                        pltpu.VMEM((tpad, 1), F32), pltpu.VMEM((rows, tpad), F32)],
    )
    return pl.pallas_call(
        functools.partial(_fox_decode_kernel, t_new=t_new, pps=pps),
        grid_spec=grid_spec,
        out_shape=jax.ShapeDtypeStruct((nseq, t_new, TOK_WIDTH), F32),
        compiler_params=_cparams(("parallel", "arbitrary")),
    )(page_table, q, kn, vn, lfn, *([cache_k] * pps), *([cache_v] * pps), *([cache_lt] * pps))


def _mem_attn_prompt_kernel(xq_ref, mk_ref, mv_ref, g_ref, o_ref):
    q = _head_rms_pairs(xq_ref[0], g_ref[...]) * (MEM_HEAD_DIM ** -0.5)
    outs = []
    for pr in range(MEM_WIDTH // LANES):
        cols = slice(pr * LANES, (pr + 1) * LANES)
        qp = q[:, cols].astype(BF16)
        kp = mk_ref[0][:, cols].astype(BF16)
        vp = mv_ref[0][:, cols].astype(BF16)
        even = _half_mask(qp.shape)
        zero = jnp.zeros(qp.shape, BF16)
        res = []
        for qh in (jnp.where(even, qp, zero), jnp.where(even, zero, qp)):
            s = _dot_nt(qh, kp)
            e = jnp.exp(s - jnp.max(s, axis=-1, keepdims=True))
            pv = _dot(e.astype(BF16), vp)
            res.append(pv / jnp.sum(e, axis=-1, keepdims=True))
        outs.append(jnp.where(even, res[0], res[1]))
    o_ref[0] = jnp.concatenate(outs, axis=-1).astype(o_ref.dtype)


def _mem_attn_prompt(xq, mk, mv, g, tq):
    bsz, t, _ = xq.shape
    qblk = pl.BlockSpec((1, tq, MEM_WIDTH), lambda b, i: (b, i, 0))
    mblk = pl.BlockSpec((1, N_MEM, MEM_WIDTH), lambda b, i: (b, 0, 0))
    return pl.pallas_call(
        _mem_attn_prompt_kernel,
        grid=(bsz, t // tq),
        in_specs=[qblk, mblk, mblk, pl.BlockSpec((1, MEM_WIDTH), lambda b, i: (0, 0))],
        out_specs=qblk,
        out_shape=jax.ShapeDtypeStruct((bsz, t, MEM_WIDTH), BF16),
        compiler_params=_cparams(("parallel", "parallel")),
    )(xq, mk, mv, g)


def _mem_attn_decode_kernel(xq_ref, mk_ref, mv_ref, g_ref, o_ref, *, t_new, nbatch):
    for bi in range(nbatch):
        q = _head_rms_pairs(xq_ref[bi], g_ref[...]) * (MEM_HEAD_DIM ** -0.5)
        wq = _head_rows(q, MEM_HEADS).astype(BF16)
        kt = mk_ref[0, bi].reshape(MEM_WIDTH, N_MEM).astype(BF16)
        vt = mv_ref[0, bi].reshape(MEM_WIDTH, N_MEM).astype(BF16)
        s = _dot(wq, kt)
        e = jnp.exp(s - jnp.max(s, axis=-1, keepdims=True))
        z = _dot_nt(e.astype(BF16), vt) / jnp.sum(e, axis=-1, keepdims=True)
        o_ref[bi] = _head_diag(z, MEM_HEADS, t_new).astype(o_ref.dtype)


def _mem_attn_decode(xq, cache_mk, cache_mv, g, layer, nbatch):
    nseq, t_new, _ = xq.shape
    qblk = pl.BlockSpec((nbatch, t_new, MEM_WIDTH), lambda b: (b, 0, 0))
    mblk = pl.BlockSpec((1, nbatch, MEM_HEADS, MEM_HEAD_DIM, N_MEM), lambda b: (layer, b, 0, 0, 0))
    return pl.pallas_call(
        functools.partial(_mem_attn_decode_kernel, t_new=t_new, nbatch=nbatch),
        grid=(nseq // nbatch,),
        in_specs=[qblk, mblk, mblk, pl.BlockSpec((1, MEM_WIDTH), lambda b: (0, 0))],
        out_specs=qblk,
        out_shape=jax.ShapeDtypeStruct((nseq, t_new, MEM_WIDTH), F32),
        compiler_params=_cparams(("parallel",)),
    )(xq, cache_mk, cache_mv, g)


FF_CHUNK = 1024


def _out_mlp_kernel(x_ref, o_ref, xo_ref, wo_ref, wm_ref, g_ref, wu_ref, wd_ref, y_ref):
    y_ref[...] = (x_ref[...] + _dot(o_ref[...].astype(BF16), wo_ref[...])
                  + _dot(xo_ref[...].astype(BF16), wm_ref[...]))
    y1 = y_ref[...]
    xn = _rms(y1, g_ref[...]).astype(BF16)
    acc = y1
    for c in range(D_FF // FF_CHUNK):
        cols = slice(c * FF_CHUNK, (c + 1) * FF_CHUNK)
        h = jnp.maximum(_dot(xn, wu_ref[:, cols]), 0.0)
        acc = acc + _dot((h * h).astype(BF16), wd_ref[cols, :])
    y_ref[...] = acc


def _out_mlp(x, o, xo, wo, wm, g, wu, wd, tm):
    m = x.shape[0]
    row = lambda n: pl.BlockSpec((tm, n), lambda i: (i, 0))
    full = lambda a: pl.BlockSpec(a.shape, lambda i: (0,) * a.ndim, pipeline_mode=pl.Buffered(1))
    return pl.pallas_call(
        _out_mlp_kernel,
        grid=(m // tm,),
        in_specs=[row(D_MODEL), row(o.shape[1]), row(MEM_WIDTH), full(wo), full(wm), full(g), full(wu), full(wd)],
        out_specs=row(D_MODEL),
        out_shape=jax.ShapeDtypeStruct((m, D_MODEL), F32),
        compiler_params=_cparams(("parallel",)),
    )(x, o, xo, wo, wm, g, wu, wd)


def _pad_heads(w, heads, d, dpad):
    lead = w.shape[:-1]
    w = w.reshape(lead + (heads, d))
    w = jnp.pad(w, [(0, 0)] * len(lead) + [(0, 0), (0, dpad - d)])
    return w.reshape(lead + (heads * dpad,))


def _gla_weights(w_in, w_a2, b_a, b_r, o_gain, w_out):
    hk = GLA_HEADS * GLA_DK
    splits = np.cumsum((hk, hk, TOK_WIDTH, GLA_GATE_RANK, TOK_WIDTH))
    wq, wk, wv, wa, wr, wx = jnp.split(w_in, [int(s) for s in splits], axis=1)
    w = jnp.concatenate([
        _pad_heads(wq, GLA_HEADS, GLA_DK, GLA_DK_PAD), _pad_heads(wk, GLA_HEADS, GLA_DK, GLA_DK_PAD),
        _pad_heads(wv, GLA_HEADS, GLA_DV, GLA_DV_PAD), _pad_heads(wr, GLA_HEADS, GLA_DV, GLA_DV_PAD),
        wx, jnp.pad(wa, ((0, 0), (0, LANES - GLA_GATE_RANK)))], axis=1).astype(BF16)
    wa2 = jnp.pad(_pad_heads(w_a2, GLA_HEADS, GLA_DK, GLA_DK_PAD), ((0, LANES - GLA_GATE_RANK), (0, 0))).astype(BF16)
    ba = _pad_heads(b_a[None, :], GLA_HEADS, GLA_DK, GLA_DK_PAD)
    br = _pad_heads(b_r[None, :], GLA_HEADS, GLA_DV, GLA_DV_PAD)
    og = jnp.pad(o_gain[None, :], ((0, 0), (0, GLA_DV_PAD - GLA_DV)))
    wo = _pad_heads(w_out[:TOK_WIDTH].T, GLA_HEADS, GLA_DV, GLA_DV_PAD).T.astype(BF16)
    wm = w_out[TOK_WIDTH:].astype(BF16)
    return w, wa2, ba, br, og, wo, wm


def _fox_weights(w_in, b_f, q_gain, k_gain, w_out):
    splits = np.cumsum((TOK_WIDTH, TOK_WIDTH, TOK_WIDTH, FOX_HEADS))
    wq, wk, wv, wf, wx = jnp.split(w_in, [int(s) for s in splits], axis=1)
    w = jnp.concatenate([wq, wk, wv, wx, jnp.pad(wf, ((0, 0), (0, LANES - FOX_HEADS)))], axis=1).astype(BF16)
    bfp = jnp.pad(b_f[None, :], ((0, 0), (0, LANES - FOX_HEADS)))
    qg = jnp.tile(q_gain, FOX_HEADS)[None, :]
    kg = jnp.tile(k_gain, FOX_HEADS)[None, :]
    return w, bfp, qg, kg, w_out[:TOK_WIDTH].astype(BF16), w_out[TOK_WIDTH:].astype(BF16)


TM = 512
GLA_CHUNK_TOKENS = 64
GLA_BLOCK_TOKENS = 256
GLA_DECODE_BATCH = 4
MEM_DECODE_BATCH = 8
FOX_TQ = 512
FOX_TK = 1024
FOX_PAGES_PER_STEP = 4
CUMSUM_ROWS = 512


def kernel(x_prompt, x_sample, mem_prompt, cache_fox_k, cache_fox_v, cache_fox_logf, page_table, state_gla, cache_mem_k, cache_mem_v, norm_mix, norm_mlp, norm_mem, gla_w_in, gla_w_a2, gla_b_a, gla_b_r, gla_o_norm, gla_w_out, fox_w_in, fox_b_f, fox_q_norm, fox_k_norm, fox_w_out, mem_w_kv, mem_q_norm, mem_k_norm, mlp_w_up, mlp_w_down):
    nb, seq, _ = x_prompt.shape
    ns, tdec, _ = x_sample.shape
    mp, ms = nb * seq, ns * tdec
    yp = x_prompt.reshape(mp, D_MODEL)
    ys = x_sample.reshape(ms, D_MODEL)
    mem2 = mem_prompt.reshape(nb * N_MEM, D_MODEL)
    clt = jnp.pad(jnp.swapaxes(cache_fox_logf, 2, 3), ((0, 0), (0, 0), (0, 2 * SUBLANES - FOX_HEADS), (0, 0)))
    to_t = lambda a: jnp.transpose(a, (0, 1, 3, 4, 2))
    ckt, cvt, cmk, cmv = to_t(cache_fox_k), to_t(cache_fox_v), to_t(cache_mem_k), to_t(cache_mem_v)
    tms = min(TM, ms)

    fk_p, fv_p, fl_p, gs_p, mk_all, mv_all = [], [], [], [], [], []
    fk_s, fv_s, fl_s, gs_s = [], [], [], []
    for l in range(DEPTH):
        i = l // 2
        g_mix = norm_mix[l][None, :]
        mqg = jnp.tile(mem_q_norm[l], MEM_HEADS)[None, :]
        mkg = jnp.tile(mem_k_norm[l], MEM_HEADS)[None, :]
        mk_p, mv_p = _mem_kv(mem2, norm_mem[l][None, :], mem_w_kv[l].astype(BF16), mkg)
        mk_all.append(mk_p.reshape(nb, N_MEM, MEM_HEADS, MEM_HEAD_DIM))
        mv_all.append(mv_p.reshape(nb, N_MEM, MEM_HEADS, MEM_HEAD_DIM))
        mk_p = mk_p.reshape(nb, N_MEM, MEM_WIDTH)
        mv_p = mv_p.reshape(nb, N_MEM, MEM_WIDTH)
        if l % 2 == 0:
            w, wa2, ba, br, og, wo, wm = _gla_weights(gla_w_in[i], gla_w_a2[i], gla_b_a[i], gla_b_r[i],
                                                      gla_o_norm[i], gla_w_out[i])
            outs = []
            for x2, bsz, t, tm, s0, nbat in ((yp, nb, seq, TM, None, nb),
                                             (ys, ns, tdec, tms, state_gla[i], math.gcd(ns, GLA_DECODE_BATCH))):
                q, k, v, r, xq, la = _gla_proj(x2, g_mix, w, wa2, ba, tm)
                r3 = lambda a: a.reshape(bsz, t, a.shape[-1])
                o, s_new = _gla_mix(r3(q), r3(k), r3(la), r3(v), r3(r), og, br, s0,
                                    chunk=GLA_CHUNK_TOKENS if t >= GLA_CHUNK_TOKENS else 2 * SUBLANES,
                                    tblk=GLA_BLOCK_TOKENS, nbatch=nbat)
                outs.append((o.reshape(bsz * t, -1), r3(xq), s_new))
            (o_p, xq_p, sp), (o_s, xq_s, ss) = outs
            gs_p.append(sp)
            gs_s.append(ss)
        else:
            w, bfp, qg, kg, wo, wm = _fox_weights(fox_w_in[i], fox_b_f[i], fox_q_norm[i], fox_k_norm[i], fox_w_out[i])
            qb, kf, kb, vf, vb, xq_p, lf = _fox_proj(yp, g_mix, w, qg, kg, bfp, TM)
            r3 = lambda a: a.reshape(nb, seq, a.shape[-1])
            c = _cumsum_rows(r3(lf), CUMSUM_ROWS)
            ct = jnp.swapaxes(c[:, :, :2 * SUBLANES], 1, 2)
            o_p = _fox_flash(r3(qb), r3(kb), r3(vb), c, ct, min(FOX_TQ, seq), min(FOX_TK, seq)).reshape(mp, TOK_WIDTH)
            xq_p = r3(xq_p)
            fk_p.append(kf.reshape(nb, seq, FOX_HEADS, FOX_HEAD_DIM))
            fv_p.append(vf.reshape(nb, seq, FOX_HEADS, FOX_HEAD_DIM))
            fl_p.append(lf[:, :FOX_HEADS].reshape(nb, seq, FOX_HEADS))
            qb, kf, kb, vf, vb, xq_s, lf = _fox_proj(ys, g_mix, w, qg, kg, bfp, tms)
            r3 = lambda a: a.reshape(ns, tdec, a.shape[-1])
            o_s = _fox_decode(r3(qb.astype(F32)), r3(kf), r3(vf), r3(lf), ckt, cvt, clt, page_table,
                              i, FOX_PAGES_PER_STEP).reshape(ms, TOK_WIDTH)
            xq_s = r3(xq_s)
            fk_s.append(kf.reshape(ns, tdec, FOX_HEADS, FOX_HEAD_DIM))
            fv_s.append(vf.reshape(ns, tdec, FOX_HEADS, FOX_HEAD_DIM))
            fl_s.append(lf[:, :FOX_HEADS].reshape(ns, tdec, FOX_HEADS))
        xo_p = _mem_attn_prompt(xq_p, mk_p, mv_p, mqg, TM).reshape(mp, MEM_WIDTH)
        xo_s = _mem_attn_decode(xq_s, cmk, cmv, mqg, l, math.gcd(ns, MEM_DECODE_BATCH)).reshape(ms, MEM_WIDTH)
        g_mlp = norm_mlp[l][None, :]
        wu = mlp_w_up[l].astype(BF16)
        wd = mlp_w_down[l].astype(BF16)
        yp = _out_mlp(yp, o_p, xo_p, wo, wm, g_mlp, wu, wd, TM)
        ys = _out_mlp(ys, o_s, xo_s, wo, wm, g_mlp, wu, wd, tms)
    return (yp.reshape(nb, seq, D_MODEL), ys.reshape(ns, tdec, D_MODEL),
            jnp.stack(fk_p, axis=1), jnp.stack(fv_p, axis=1), jnp.stack(fl_p, axis=1),
            jnp.stack(gs_p, axis=0), jnp.stack(mk_all, axis=0), jnp.stack(mv_all, axis=0),
            jnp.stack(fk_s, axis=1), jnp.stack(fv_s, axis=1), jnp.stack(fl_s, axis=1),
            jnp.stack(gs_s, axis=0))
```

```python
import functools
import math

import numpy as np
import jax
import jax.numpy as jnp
from jax import lax
from jax.experimental import pallas as pl
from jax.experimental.pallas import tpu as pltpu

F32 = jnp.float32
BF16 = jnp.bfloat16

D_MODEL = 1024
DEPTH = 4
PAGE_SIZE = 128
TOK_WIDTH = 768
MEM_WIDTH = 256
N_MEM = 256
MEM_HEADS = 4
MEM_HEAD_DIM = 64
GLA_HEADS = 4
GLA_DV = 192
GLA_DK = 96
GLA_GATE_RANK = 16
GLA_TAU = 16.0
FOX_HEAD_DIM = 64
FOX_HEADS = 12
D_FF = 4 * D_MODEL
EPS = 1e-6

LANES = 128
SUBLANES = 8
GLA_DK_PAD = 128
GLA_DV_PAD = 256
NEG_BIG = -1e30
VMEM_LIMIT = 56 * 1024 * 1024


def _cparams(sem):
    return pltpu.CompilerParams(dimension_semantics=sem, vmem_limit_bytes=VMEM_LIMIT)


def _resident(a):
    return pl.BlockSpec(a.shape, lambda i: (0,) * a.ndim, pipeline_mode=pl.Buffered(1))


def _dot(a, b):
    return jnp.dot(a, b, preferred_element_type=F32)


def _dot_nt(a, b):
    return lax.dot_general(a, b, (((1,), (1,)), ((), ())), preferred_element_type=F32)


def _dot_tn(a, b):
    return lax.dot_general(a, b, (((0,), (0,)), ((), ())), preferred_element_type=F32)


def _split3(x):
    hi = x.astype(BF16)
    r1 = x - hi.astype(F32)
    mid = r1.astype(BF16)
    lo = (r1 - mid.astype(F32)).astype(BF16)
    return hi, mid, lo


def _dot01(m01, x):
    hi, mid, lo = _split3(x)
    return (_dot(m01, lo) + _dot(m01, mid)) + _dot(m01, hi)


def _rms(x, g):
    return x * lax.rsqrt(jnp.mean(x * x, axis=-1, keepdims=True) + EPS) * g


def _log_sigmoid(x):
    return jnp.minimum(x, 0.0) - jnp.log(1.0 + jnp.exp(-jnp.abs(x)))


def _half_mask(shape):
    lane = lax.broadcasted_iota(jnp.int32, shape, len(shape) - 1)
    return (lane & FOX_HEAD_DIM) == 0


def _log2(n):
    k = int(n).bit_length() - 1
    assert 1 << k == n, n
    return k


def _ones_where(cond):
    return jnp.where(cond, 1.0, 0.0).astype(BF16)


def _head_rms_pairs(x, gain):
    n = x.shape[-1] // LANES
    outs = []
    for c in range(n):
        xc = x[:, c * LANES:(c + 1) * LANES]
        sq = xc * xc
        even = _half_mask(xc.shape)
        s_e = jnp.sum(jnp.where(even, sq, 0.0), axis=-1, keepdims=True)
        s_o = jnp.sum(jnp.where(even, 0.0, sq), axis=-1, keepdims=True)
        rs = jnp.where(even, lax.rsqrt(s_e / FOX_HEAD_DIM + EPS), lax.rsqrt(s_o / FOX_HEAD_DIM + EPS))
        outs.append(xc * rs)
    return jnp.concatenate(outs, axis=-1) * gain


GLA_COLS = (GLA_HEADS * GLA_DK_PAD, GLA_HEADS * GLA_DK_PAD, GLA_HEADS * GLA_DV_PAD,
            GLA_HEADS * GLA_DV_PAD, MEM_WIDTH, LANES)
GLA_OFF = tuple(int(v) for v in np.cumsum((0,) + GLA_COLS))


def _gla_proj_kernel(x_ref, g_ref, w_ref, wa_ref, ba_ref, q_ref, k_ref, v_ref, r_ref, xq_ref, la_ref):
    xn = _rms(x_ref[...], g_ref[...]).astype(BF16)
    y = _dot(xn, w_ref[...])
    o = GLA_OFF
    q_ref[...] = y[:, o[0]:o[1]] * (GLA_DK ** -0.5)
    k_ref[...] = y[:, o[1]:o[2]]
    v_ref[...] = y[:, o[2]:o[3]].astype(v_ref.dtype)
    r_ref[...] = y[:, o[3]:o[4]]
    xq_ref[...] = y[:, o[4]:o[5]]
    a = _dot(y[:, o[5]:o[6]].astype(BF16), wa_ref[...]) + ba_ref[...]
    la_ref[...] = _log_sigmoid(a) * (1.0 / GLA_TAU)


def _gla_proj(x, g, w, wa, ba, tm):
    m = x.shape[0]
    row = lambda n: pl.BlockSpec((tm, n), lambda i: (i, 0))
    full = _resident
    hk, hv = GLA_HEADS * GLA_DK_PAD, GLA_HEADS * GLA_DV_PAD
    out_shape = (jax.ShapeDtypeStruct((m, hk), F32), jax.ShapeDtypeStruct((m, hk), F32),
                 jax.ShapeDtypeStruct((m, hv), F32), jax.ShapeDtypeStruct((m, hv), F32),
                 jax.ShapeDtypeStruct((m, MEM_WIDTH), F32), jax.ShapeDtypeStruct((m, hk), F32))
    return pl.pallas_call(
        _gla_proj_kernel,
        grid=(m // tm,),
        in_specs=[row(D_MODEL), full(g), full(w), full(wa), full(ba)],
        out_specs=(row(hk), row(hk), row(hv), row(hv), row(MEM_WIDTH), row(hk)),
        out_shape=out_shape,
        compiler_params=_cparams(("parallel",)),
    )(x, g, w, wa, ba)


FOX_PAD_WIDTH = FOX_HEADS * LANES
FOX_AUX = 3
LOG2E = 1.4426950408889634
FOX_COLS = (FOX_PAD_WIDTH, FOX_PAD_WIDTH, TOK_WIDTH, MEM_WIDTH, LANES)
FOX_OFF = tuple(int(v) for v in np.cumsum((0,) + FOX_COLS))


def _head_rms_padded(x, gain):
    outs = []
    for h in range(x.shape[-1] // LANES):
        xh = x[:, h * LANES:(h + 1) * LANES]
        ms = jnp.sum(xh * xh, axis=-1, keepdims=True) * (1.0 / FOX_HEAD_DIM)
        outs.append(xh * lax.rsqrt(ms + EPS))
    return jnp.concatenate(outs, axis=-1) * gain


def _fox_proj_kernel(x_ref, g_ref, w_ref, qg_ref, kg_ref, bf_ref, qa_ref, kp_ref, vf_ref, vt_ref, xq_ref, lf_ref):
    xn = _rms(x_ref[...], g_ref[...]).astype(BF16)
    y = _dot(xn, w_ref[...])
    o = FOX_OFF
    q = _head_rms_padded(y[:, o[0]:o[1]], qg_ref[...]) * (FOX_HEAD_DIM ** -0.5 * LOG2E)
    lane = lax.broadcasted_iota(jnp.int32, q.shape, 1) & (LANES - 1)
    aux = jnp.logical_and(lane >= FOX_HEAD_DIM, lane < FOX_HEAD_DIM + FOX_AUX)
    qa_ref[...] = jnp.where(aux, 1.0, q).astype(BF16)
    kp_ref[...] = _head_rms_padded(y[:, o[1]:o[2]], kg_ref[...])
    v = y[:, o[2]:o[3]]
    vf_ref[...] = v
    vt_ref[...] = v.T.astype(BF16)
    xq_ref[...] = y[:, o[3]:o[4]]
    lf_ref[...] = _log_sigmoid(y[:, o[4]:o[5]] + bf_ref[...])


def _fox_proj(x, g, w, qg, kg, bfp, tm):
    m = x.shape[0]
    row = lambda n: pl.BlockSpec((tm, n), lambda i: (i, 0))
    full = _resident
    tw, pw = TOK_WIDTH, FOX_PAD_WIDTH
    out_shape = (jax.ShapeDtypeStruct((m, pw), BF16), jax.ShapeDtypeStruct((m, pw), F32),
                 jax.ShapeDtypeStruct((m, tw), F32), jax.ShapeDtypeStruct((tw, m), BF16),
                 jax.ShapeDtypeStruct((m, MEM_WIDTH), F32), jax.ShapeDtypeStruct((m, LANES), F32))
    return pl.pallas_call(
        _fox_proj_kernel,
        grid=(m // tm,),
        in_specs=[row(D_MODEL), full(g), full(w), full(qg), full(kg), full(bfp)],
        out_specs=(row(pw), row(pw), row(tw), pl.BlockSpec((tw, tm), lambda i: (0, i)), row(MEM_WIDTH), row(LANES)),
        out_shape=out_shape,
        compiler_params=_cparams(("parallel",)),
    )(x, g, w, qg, kg, bfp)


def _mem_kv_kernel(x_ref, g_ref, w_ref, kg_ref, k_ref, v_ref):
    xn = _rms(x_ref[...], g_ref[...]).astype(BF16)
    y = _dot(xn, w_ref[...])
    k_ref[...] = _head_rms_pairs(y[:, :MEM_WIDTH], kg_ref[...])
    v_ref[...] = y[:, MEM_WIDTH:]


def _mem_kv(x, g, w, kg):
    m = x.shape[0]
    full = _resident
    out = jax.ShapeDtypeStruct((m, MEM_WIDTH), F32)
    return pl.pallas_call(
        _mem_kv_kernel,
        grid=(1,),
        in_specs=[full(x), full(g), full(w), full(kg)],
        out_specs=(pl.BlockSpec((m, MEM_WIDTH), lambda i: (0, 0)),) * 2,
        out_shape=(out, out),
        compiler_params=_cparams(("arbitrary",)),
    )(x, g, w, kg)


def _gla_levels(c):
    out, h = [], c // 2
    while h >= SUBLANES:
        out.append(h)
        h //= 2
    return out


def _gla_cum_matrix(c):
    t = np.arange(c)
    blocks = [(t[None, :] <= t[:, None])]
    for h in _gla_levels(c):
        anchor = (t // (2 * h)) * (2 * h) + h - 1
        blocks.append(t[None, :] <= anchor[:, None])
    return jnp.asarray(np.concatenate(blocks, axis=0).astype(np.float32), dtype=BF16)


def _gla_kernel(*refs, chunk, n_sub, t_valid, has_s0, nbatch):
    if has_s0:
        cum_ref, q_ref, k_ref, la_ref, v_ref, r_ref, og_ref, br_ref, s0_ref, o_ref, sout_ref, st_ref, pad_ref = refs
    else:
        cum_ref, q_ref, k_ref, la_ref, v_ref, r_ref, og_ref, br_ref, o_ref, sout_ref, st_ref, pad_ref = refs
        s0_ref = None
    c = chunk
    levels = _gla_levels(c)
    nb = c // SUBLANES
    step = pl.program_id(1)
    chains = [(bi, h) for bi in range(nbatch) for h in range(GLA_HEADS)]

    @pl.when(step == 0)
    def _():
        for ci, (bi, h) in enumerate(chains):
            if has_s0:
                pad_ref[...] = jnp.zeros(pad_ref.shape, F32)
                pad_ref[:GLA_DK, :GLA_DV] = s0_ref[bi, h]
                st_ref[ci] = pad_ref[...].T
            else:
                st_ref[ci] = jnp.zeros(st_ref.shape[1:], F32)

    def load(ref, bi, rows, h, width):
        x = ref[bi, rows, h * width:(h + 1) * width]
        if t_valid < c:
            x = jnp.concatenate([x, jnp.zeros((c - t_valid, width), F32)], axis=0)
        return x

    def do_chunk(ci, bi, h, rows):
        q = load(q_ref, bi, rows, h, GLA_DK_PAD)
        k = load(k_ref, bi, rows, h, GLA_DK_PAD)
        la = load(la_ref, bi, rows, h, GLA_DK_PAD)
        v = load(v_ref, bi, rows, h, GLA_DV_PAD)
        vb = v.astype(BF16)
        cum = _dot01(cum_ref[...], la)
        b = cum[:c]
        b_last = b[c - 1:c, :]
        st = st_ref[ci]
        o = _dot_nt((q * jnp.exp(b)).astype(BF16), st.astype(BF16))
        if levels:
            ti = lax.broadcasted_iota(jnp.int32, (c, c), 0)
            si = lax.broadcasted_iota(jnp.int32, (c, c), 1)
            row = lax.broadcasted_iota(jnp.int32, (c, GLA_DK_PAD), 0)
            attn = jnp.zeros((c, c), F32)
            for li, half in enumerate(levels):
                anchor = cum[(li + 1) * c:(li + 2) * c]
                gdec = jnp.exp(-jnp.abs(b - anchor))
                first = (row & half) == 0
                ql = jnp.where(first, 0.0, q * gdec).astype(BF16)
                kl = jnp.where(first, k * gdec, 0.0).astype(BF16)
                a = _dot_nt(ql, kl)
                sh = _log2(2 * half)
                attn = attn + jnp.where((ti >> sh) == (si >> sh), a, 0.0)
            o = o + _dot(attn.astype(BF16), vb)
        q3 = q.reshape(nb, SUBLANES, GLA_DK_PAD)
        k3 = k.reshape(nb, SUBLANES, GLA_DK_PAD)
        b3 = b.reshape(nb, SUBLANES, GLA_DK_PAD)
        v3 = v.reshape(nb, SUBLANES, GLA_DV_PAD)
        sub = lax.broadcasted_iota(jnp.int32, (nb, SUBLANES, GLA_DK_PAD), 1)
        od = jnp.zeros((nb, SUBLANES, GLA_DV_PAD), F32)
        for j in range(SUBLANES):
            kj = jnp.broadcast_to(k3[:, j:j + 1, :], k3.shape)
            bj = jnp.broadcast_to(b3[:, j:j + 1, :], b3.shape)
            e = jnp.exp(jnp.where(sub >= j, b3 - bj, -jnp.inf))
            col = jnp.sum(q3 * kj * e, axis=-1, keepdims=True)
            od = od + col * jnp.broadcast_to(v3[:, j:j + 1, :], v3.shape)
        o = o + od.reshape(c, GLA_DV_PAD)
        kg = (k * jnp.exp(b_last - b)).astype(BF16)
        st_ref[ci] = st * jnp.exp(b_last) + _dot_tn(vb, kg)
        ms = jnp.sum(o * o, axis=-1, keepdims=True) * (1.0 / GLA_DV)
        on = o * lax.rsqrt(ms + EPS) * og_ref[...]
        r = load(r_ref, bi, rows, h, GLA_DV_PAD) + br_ref[:, h * GLA_DV_PAD:(h + 1) * GLA_DV_PAD]
        res = on * (r * (1.0 / (1.0 + jnp.exp(-r))))
        return res[:t_valid] if t_valid < c else res

    def all_chains(rows):
        for ci, (bi, h) in enumerate(chains):
            res = do_chunk(ci, bi, h, rows)
            o_ref[bi, rows, h * GLA_DV_PAD:(h + 1) * GLA_DV_PAD] = res.astype(o_ref.dtype)

    if n_sub == 1:
        all_chains(slice(None))
    else:
        def body(i, carry):
            all_chains(pl.ds(pl.multiple_of(i * c, c), c))
            return carry
        lax.fori_loop(0, n_sub, body, 0)

    @pl.when(step == pl.num_programs(1) - 1)
    def _():
        for ci, (bi, h) in enumerate(chains):
            pad_ref[...] = st_ref[ci].T
            sout_ref[bi, h] = pad_ref[:GLA_DK, :GLA_DV]


def _gla_mix(q, k, la, v, r, og, br, s0, *, chunk, tblk, nbatch):
    bsz, t, _ = q.shape
    has_s0 = s0 is not None
    if t < chunk:
        t_valid, n_sub, tblk = t, 1, t
    else:
        t_valid, n_sub = chunk, tblk // chunk
    cum = _gla_cum_matrix(chunk)
    hk, hv = GLA_HEADS * GLA_DK_PAD, GLA_HEADS * GLA_DV_PAD
    kblk = pl.BlockSpec((nbatch, tblk, hk), lambda b, i: (b, i, 0))
    vblk = pl.BlockSpec((nbatch, tblk, hv), lambda b, i: (b, i, 0))
    sblk = pl.BlockSpec((nbatch, GLA_HEADS, GLA_DK, GLA_DV), lambda b, i: (b, 0, 0, 0))
    in_specs = [pl.BlockSpec(cum.shape, lambda b, i: (0, 0)), kblk, kblk, kblk, vblk, vblk,
                pl.BlockSpec((1, GLA_DV_PAD), lambda b, i: (0, 0)),
                pl.BlockSpec((1, hv), lambda b, i: (0, 0))]
    args = [cum, q, k, la, v, r, og, br]
    if has_s0:
        in_specs.append(sblk)
        args.append(s0)
    kern = functools.partial(_gla_kernel, chunk=chunk, n_sub=n_sub, t_valid=t_valid, has_s0=has_s0, nbatch=nbatch)
    return pl.pallas_call(
        kern,
        grid=(bsz // nbatch, t // tblk),
        in_specs=in_specs,
        out_specs=(vblk, sblk),
        out_shape=(jax.ShapeDtypeStruct((bsz, t, hv), BF16 if t >= chunk else F32),
                   jax.ShapeDtypeStruct((bsz, GLA_HEADS, GLA_DK, GLA_DV), F32)),
        scratch_shapes=[pltpu.VMEM((nbatch * GLA_HEADS, GLA_DV_PAD, GLA_DK_PAD), F32),
                        pltpu.VMEM((GLA_DK_PAD, GLA_DV_PAD), F32)],
        compiler_params=_cparams(("parallel", "arbitrary")),
    )(*args)


def _key_bias_kernel(tri_ref, lf_ref, kp_ref, ka_ref, carry_ref):
    @pl.when(pl.program_id(1) == 0)
    def _():
        carry_ref[...] = jnp.zeros(carry_ref.shape, F32)
    c = _dot01(tri_ref[...], lf_ref[0]) + carry_ref[...]
    carry_ref[...] = c[c.shape[0] - 1:, :]
    nbias = c * (-LOG2E)
    hlane = lax.broadcasted_iota(jnp.int32, c.shape, 1)
    lane = lax.broadcasted_iota(jnp.int32, c.shape, 1)
    for h in range(FOX_HEADS):
        col = jnp.sum(jnp.where(hlane == h, nbias, 0.0), axis=-1, keepdims=True)
        hi = col.astype(BF16).astype(F32)
        mid = (col - hi).astype(BF16).astype(F32)
        lo = col - hi - mid
        kh = kp_ref[0, :, h * LANES:(h + 1) * LANES]
        kh = jnp.where(lane == FOX_HEAD_DIM, hi, kh)
        kh = jnp.where(lane == FOX_HEAD_DIM + 1, mid, kh)
        kh = jnp.where(lane == FOX_HEAD_DIM + 2, lo, kh)
        ka_ref[0, :, h * LANES:(h + 1) * LANES] = kh.astype(BF16)


def _key_bias(lf, kp, tc):
    bsz, t, n = lf.shape
    tri = jnp.asarray(np.tril(np.ones((tc, tc), np.float32)), dtype=BF16)
    kblk = pl.BlockSpec((1, tc, FOX_PAD_WIDTH), lambda b, i: (b, i, 0))
    return pl.pallas_call(
        _key_bias_kernel,
        grid=(bsz, t // tc),
        in_specs=[pl.BlockSpec((tc, tc), lambda b, i: (0, 0)), pl.BlockSpec((1, tc, n), lambda b, i: (b, i, 0)), kblk],
        out_specs=kblk,
        out_shape=jax.ShapeDtypeStruct((bsz, t, FOX_PAD_WIDTH), BF16),
        scratch_shapes=[pltpu.VMEM((1, n), F32)],
        compiler_params=_cparams(("parallel", "arbitrary")),
    )(tri, lf, kp)


def _fox_flash_kernel(qi_ref, kj_ref, q_ref, k_ref, vt_ref, o_ref, m_ref, l_ref, acc_ref, *, tq, tk):
    qi = qi_ref[pl.program_id(2)]
    kj = kj_ref[pl.program_id(2)]
    last = ((qi + 1) * tq - 1) // tk

    @pl.when(kj == 0)
    def _():
        m_ref[...] = jnp.full(m_ref.shape, NEG_BIG, F32)
        l_ref[...] = jnp.zeros(l_ref.shape, F32)
        acc_ref[...] = jnp.zeros(acc_ref.shape, F32)

    def update(masked):
        vt = vt_ref[...]
        if masked:
            keyi = kj * tk + lax.broadcasted_iota(jnp.int32, (tk, tq), 0)
            qryi = qi * tq + lax.broadcasted_iota(jnp.int32, (tk, tq), 1)
            keep = keyi <= qryi
        for hh in range(2):
            st = _dot_nt(k_ref[0, :, hh * LANES:(hh + 1) * LANES], q_ref[0, :, hh * LANES:(hh + 1) * LANES])
            if masked:
                st = jnp.where(keep, st, NEG_BIG)
            m_old = m_ref[hh]
            m_new = jnp.maximum(m_old, jnp.max(st, axis=0, keepdims=True))
            alpha = jnp.exp2(m_old - m_new)
            p = jnp.exp2(st - m_new)
            l_ref[hh] = alpha * l_ref[hh] + jnp.sum(p, axis=0, keepdims=True)
            m_ref[hh] = m_new
            rows = slice(hh * FOX_HEAD_DIM, (hh + 1) * FOX_HEAD_DIM)
            acc_ref[rows, :] = alpha * acc_ref[rows, :] + _dot(vt, p.astype(BF16))[rows, :]

    crosses = (kj + 1) * tk - 1 > qi * tq

    @pl.when(crosses)
    def _():
        update(True)

    @pl.when(jnp.logical_not(crosses))
    def _():
        update(False)

    @pl.when(kj == last)
    def _():
        hd = FOX_HEAD_DIM
        ot = jnp.concatenate([acc_ref[:hd, :] / l_ref[0], acc_ref[hd:, :] / l_ref[1]], axis=0)
        o_ref[0] = ot.T.astype(o_ref.dtype)


def _fox_flash(q, k, vt, tq, tk):
    bsz, t, _ = q.shape
    npair = FOX_HEADS // 2
    pairs = [(i, j) for i in range(t // tq) for j in range(((i + 1) * tq - 1) // tk + 1)]
    qi_tbl = jnp.asarray([i for i, _ in pairs], jnp.int32)
    kj_tbl = jnp.asarray([j for _, j in pairs], jnp.int32)
    qblk = pl.BlockSpec((1, tq, 2 * LANES), lambda b, p, s, qi, kj: (b, qi[s], p))
    kblk = pl.BlockSpec((1, tk, 2 * LANES), lambda b, p, s, qi, kj: (b, kj[s], p))
    grid_spec = pltpu.PrefetchScalarGridSpec(
        num_scalar_prefetch=2,
        grid=(bsz, npair, len(pairs)),
        in_specs=[qblk, kblk, pl.BlockSpec((LANES, tk), lambda b, p, s, qi, kj: (p, b * (t // tk) + kj[s]))],
        out_specs=pl.BlockSpec((1, tq, LANES), lambda b, p, s, qi, kj: (b, qi[s], p)),
        scratch_shapes=[pltpu.VMEM((2, 1, tq), F32), pltpu.VMEM((2, 1, tq), F32), pltpu.VMEM((LANES, tq), F32)],
    )
    return pl.pallas_call(
        functools.partial(_fox_flash_kernel, tq=tq, tk=tk),
        grid_spec=grid_spec,
        out_shape=jax.ShapeDtypeStruct((bsz, t, TOK_WIDTH), BF16),
        compiler_params=_cparams(("parallel", "parallel", "arbitrary")),
    )(qi_tbl, kj_tbl, q, k, vt)


def _head_rows(x, heads):
    lane = lax.broadcasted_iota(jnp.int32, x.shape, 1) >> _log2(FOX_HEAD_DIM)
    zero = jnp.zeros(x.shape, x.dtype)
    return jnp.concatenate([jnp.where(lane == h, x, zero) for h in range(heads)], axis=0)


def _head_diag(z, heads, t):
    lane = lax.broadcasted_iota(jnp.int32, (t, z.shape[1]), 1) >> _log2(FOX_HEAD_DIM)
    out = jnp.zeros((t, z.shape[1]), z.dtype)
    for h in range(heads):
        out = jnp.where(lane == h, z[h * t:(h + 1) * t, :], out)
    return out


def _pad_rows(x, n):
    return jnp.concatenate([x, jnp.zeros((n - x.shape[0], x.shape[1]), x.dtype)], axis=0)


def _fox_decode_kernel(pt_ref, q_ref, kn_ref, vn_ref, lfn_ref, *rest, t_new, pps):
    del pt_ref
    kc_refs, vc_refs, lc_refs = rest[:pps], rest[pps:2 * pps], rest[2 * pps:3 * pps]
    o_ref, wq_ref, m_ref, l_ref, acc_ref, carry_ref, cnk_ref = rest[3 * pps:]
    p = pl.program_id(1)
    npg = pl.num_programs(1)
    heads = FOX_HEADS
    rows = heads * t_new
    tpad = 2 * SUBLANES

    @pl.when(p == 0)
    def _():
        wq_ref[...] = _head_rows(q_ref[0], heads).astype(BF16)
        m_ref[...] = jnp.full(m_ref.shape, NEG_BIG, F32)
        l_ref[...] = jnp.zeros(l_ref.shape, F32)
        acc_ref[...] = jnp.zeros(acc_ref.shape, F32)
        carry_ref[...] = jnp.zeros(carry_ref.shape, F32)
        lfn = _pad_rows(lfn_ref[0] * LOG2E, tpad)
        ti = lax.broadcasted_iota(jnp.int32, (tpad, tpad), 0)
        si = lax.broadcasted_iota(jnp.int32, (tpad, tpad), 1)
        cs = _dot01(_ones_where(si <= ti), lfn)
        rh = lax.broadcasted_iota(jnp.int32, (rows, LANES), 0) >> _log2(t_new)
        lh = lax.broadcasted_iota(jnp.int32, (rows, LANES), 1)
        hsel = _ones_where(rh == lh)
        hi, mid, lo = _split3(cs)
        cnk_ref[...] = (_dot_nt(hsel, lo) + _dot_nt(hsel, mid)) + _dot_nt(hsel, hi)

    def online(s, pv_of):
        m_old = m_ref[...]
        m_new = jnp.maximum(m_old, jnp.max(s, axis=-1, keepdims=True))
        alpha = jnp.exp2(m_old - m_new)
        pr = jnp.exp2(s - m_new)
        l_ref[...] = alpha * l_ref[...] + jnp.sum(pr, axis=-1, keepdims=True)
        m_ref[...] = m_new
        acc_ref[...] = alpha * acc_ref[...] + pv_of(pr.astype(BF16))

    cnk = cnk_ref[...]
    ri = lax.broadcasted_iota(jnp.int32, cnk.shape, 0) & (t_new - 1)
    ci = lax.broadcasted_iota(jnp.int32, cnk.shape, 1)
    cn_col = jnp.sum(jnp.where(ri == ci, cnk, 0.0), axis=-1, keepdims=True)

    ji = lax.broadcasted_iota(jnp.int32, (PAGE_SIZE, PAGE_SIZE), 0)
    si2 = lax.broadcasted_iota(jnp.int32, (PAGE_SIZE, PAGE_SIZE), 1)
    after = _ones_where(ji > si2)
    carry = carry_ref[...]
    parts = []
    for j in range(pps):
        lp = lc_refs[j][0, 0] * LOG2E
        suffix = _dot01_rhs(lp, after) + carry
        carry = carry + jnp.sum(lp, axis=-1, keepdims=True)
        r96 = jnp.concatenate([jnp.broadcast_to(suffix[h:h + 1, :], (t_new, PAGE_SIZE)) for h in range(heads)],
                              axis=0)
        kt = kc_refs[j][0, 0].reshape(TOK_WIDTH, PAGE_SIZE).astype(BF16)
        parts.append(_dot(wq_ref[...], kt) + r96)
    carry_ref[...] = carry

    def pv_pages(pr):
        out = None
        for j in range(pps):
            vt = vc_refs[j][0, 0].reshape(TOK_WIDTH, PAGE_SIZE).astype(BF16)
            pvj = _dot_nt(pr[:, j * PAGE_SIZE:(j + 1) * PAGE_SIZE], vt)
            out = pvj if out is None else out + pvj
        return out

    online(jnp.concatenate(parts, axis=1) + cn_col, pv_pages)

    @pl.when(p == npg - 1)
    def _():
        kn = _pad_rows(kn_ref[0], tpad).astype(BF16)
        vn = _pad_rows(vn_ref[0], tpad).astype(BF16)
        sn = jnp.where(ci <= ri, _dot_nt(wq_ref[...], kn) + cn_col - cnk, NEG_BIG)
        online(sn, lambda pr: _dot(pr, vn))
        o_ref[0] = _head_diag(acc_ref[...] / l_ref[...], heads, t_new).astype(o_ref.dtype)


def _dot01_rhs(x, m01):
    hi, mid, lo = _split3(x)
    return (_dot(lo, m01) + _dot(mid, m01)) + _dot(hi, m01)


def _fox_decode(q, kn, vn, lfn, cache_k, cache_v, cache_lt, page_table, layer, pps):
    nseq, t_new, _ = q.shape
    heads, hd = cache_k.shape[2], cache_k.shape[3]
    npg = page_table.shape[1]
    pps = math.gcd(npg, pps)
    rows = heads * t_new
    tpad = 2 * SUBLANES
    seq = lambda n: pl.BlockSpec((1, t_new, n), lambda b, p, pt: (b, 0, 0))

    def page_idx(j, ndim):
        return lambda b, p, pt: (pt[b, npg - 1 - (p * pps + j)], layer) + (0,) * (ndim - 2)

    kv_specs = [pl.BlockSpec((1, 1, heads, hd, PAGE_SIZE), page_idx(j, 5)) for j in range(pps)]
    lt_specs = [pl.BlockSpec((1, 1, tpad, PAGE_SIZE), page_idx(j, 4)) for j in range(pps)]
    grid_spec = pltpu.PrefetchScalarGridSpec(
        num_scalar_prefetch=1,
        grid=(nseq, npg // pps),
        in_specs=[seq(TOK_WIDTH), seq(TOK_WIDTH), seq(TOK_WIDTH), seq(LANES)] + kv_specs + kv_specs + lt_specs,
        out_specs=seq(TOK_WIDTH),
        scratch_shapes=[pltpu.VMEM((rows, TOK_WIDTH), BF16), pltpu.VMEM((rows, 1), F32),
                        pltpu.VMEM((rows, 1), F32), pltpu.VMEM((rows, TOK_WIDTH), F32),
                        pltpu.VMEM((tpad, 1), F32), pltpu.VMEM((rows, tpad), F32)],
    )
    return pl.pallas_call(
        functools.partial(_fox_decode_kernel, t_new=t_new, pps=pps),
        grid_spec=grid_spec,
        out_shape=jax.ShapeDtypeStruct((nseq, t_new, TOK_WIDTH), F32),
        compiler_params=_cparams(("parallel", "arbitrary")),
    )(page_table, q, kn, vn, lfn, *([cache_k] * pps), *([cache_v] * pps), *([cache_lt] * pps))


def _mem_attn_prompt_kernel(xq_ref, mk_ref, mv_ref, g_ref, o_ref):
    q = _head_rms_pairs(xq_ref[0], g_ref[...]) * (MEM_HEAD_DIM ** -0.5)
    outs = []
    for pr in range(MEM_WIDTH // LANES):
        cols = slice(pr * LANES, (pr + 1) * LANES)
        qp = q[:, cols].astype(BF16)
        kp = mk_ref[0][:, cols].astype(BF16)
        vp = mv_ref[0][:, cols].astype(BF16)
        even = _half_mask(qp.shape)
        zero = jnp.zeros(qp.shape, BF16)
        res = []
        for qh in (jnp.where(even, qp, zero), jnp.where(even, zero, qp)):
            s = _dot_nt(qh, kp)
            e = jnp.exp(s - jnp.max(s, axis=-1, keepdims=True))
            pv = _dot(e.astype(BF16), vp)
            res.append(pv / jnp.sum(e, axis=-1, keepdims=True))
        outs.append(jnp.where(even, res[0], res[1]))
    o_ref[0] = jnp.concatenate(outs, axis=-1).astype(o_ref.dtype)


def _mem_attn_prompt(xq, mk, mv, g, tq):
    bsz, t, _ = xq.shape
    qblk = pl.BlockSpec((1, tq, MEM_WIDTH), lambda b, i: (b, i, 0))
    mblk = pl.BlockSpec((1, N_MEM, MEM_WIDTH), lambda b, i: (b, 0, 0))
    return pl.pallas_call(
        _mem_attn_prompt_kernel,
        grid=(bsz, t // tq),
        in_specs=[qblk, mblk, mblk, pl.BlockSpec((1, MEM_WIDTH), lambda b, i: (0, 0))],
        out_specs=qblk,
        out_shape=jax.ShapeDtypeStruct((bsz, t, MEM_WIDTH), BF16),
        compiler_params=_cparams(("parallel", "parallel")),
    )(xq, mk, mv, g)


def _mem_attn_decode_kernel(xq_ref, mk_ref, mv_ref, g_ref, o_ref, *, t_new, nbatch):
    for bi in range(nbatch):
        q = _head_rms_pairs(xq_ref[bi], g_ref[...]) * (MEM_HEAD_DIM ** -0.5)
        wq = _head_rows(q, MEM_HEADS).astype(BF16)
        kt = mk_ref[0, bi].reshape(MEM_WIDTH, N_MEM).astype(BF16)
        vt = mv_ref[0, bi].reshape(MEM_WIDTH, N_MEM).astype(BF16)
        s = _dot(wq, kt)
        e = jnp.exp(s - jnp.max(s, axis=-1, keepdims=True))
        z = _dot_nt(e.astype(BF16), vt) / jnp.sum(e, axis=-1, keepdims=True)
        o_ref[bi] = _head_diag(z, MEM_HEADS, t_new).astype(o_ref.dtype)


def _mem_attn_decode(xq, cache_mk, cache_mv, g, layer, nbatch):
    nseq, t_new, _ = xq.shape
    qblk = pl.BlockSpec((nbatch, t_new, MEM_WIDTH), lambda b: (b, 0, 0))
    mblk = pl.BlockSpec((1, nbatch, MEM_HEADS, MEM_HEAD_DIM, N_MEM), lambda b: (layer, b, 0, 0, 0))
    return pl.pallas_call(
        functools.partial(_mem_attn_decode_kernel, t_new=t_new, nbatch=nbatch),
        grid=(nseq // nbatch,),
        in_specs=[qblk, mblk, mblk, pl.BlockSpec((1, MEM_WIDTH), lambda b: (0, 0))],
        out_specs=qblk,
        out_shape=jax.ShapeDtypeStruct((nseq, t_new, MEM_WIDTH), F32),
        compiler_params=_cparams(("parallel",)),
    )(xq, cache_mk, cache_mv, g)


FF_CHUNK = 1024


def _out_mlp_kernel(x_ref, o_ref, xo_ref, wo_ref, wm_ref, g_ref, wu_ref, wd_ref, y_ref):
    y_ref[...] = (x_ref[...] + _dot(o_ref[...].astype(BF16), wo_ref[...])
                  + _dot(xo_ref[...].astype(BF16), wm_ref[...]))
    y1 = y_ref[...]
    xn = _rms(y1, g_ref[...]).astype(BF16)
    acc = y1
    for c in range(D_FF // FF_CHUNK):
        cols = slice(c * FF_CHUNK, (c + 1) * FF_CHUNK)
        h = jnp.maximum(_dot(xn, wu_ref[:, cols]), 0.0)
        acc = acc + _dot((h * h).astype(BF16), wd_ref[cols, :])
    y_ref[...] = acc


def _out_mlp(x, o, xo, wo, wm, g, wu, wd, tm):
    m = x.shape[0]
    row = lambda n: pl.BlockSpec((tm, n), lambda i: (i, 0))
    full = lambda a: pl.BlockSpec(a.shape, lambda i: (0,) * a.ndim, pipeline_mode=pl.Buffered(1))
    return pl.pallas_call(
        _out_mlp_kernel,
        grid=(m // tm,),
        in_specs=[row(D_MODEL), row(o.shape[1]), row(MEM_WIDTH), full(wo), full(wm), full(g), full(wu), full(wd)],
        out_specs=row(D_MODEL),
        out_shape=jax.ShapeDtypeStruct((m, D_MODEL), F32),
        compiler_params=_cparams(("parallel",)),
    )(x, o, xo, wo, wm, g, wu, wd)


def _pad_heads(w, heads, d, dpad):
    lead = w.shape[:-1]
    w = w.reshape(lead + (heads, d))
    w = jnp.pad(w, [(0, 0)] * len(lead) + [(0, 0), (0, dpad - d)])
    return w.reshape(lead + (heads * dpad,))


def _gla_weights(w_in, w_a2, b_a, b_r, o_gain, w_out):
    hk = GLA_HEADS * GLA_DK
    splits = np.cumsum((hk, hk, TOK_WIDTH, GLA_GATE_RANK, TOK_WIDTH))
    wq, wk, wv, wa, wr, wx = jnp.split(w_in, [int(s) for s in splits], axis=1)
    w = jnp.concatenate([
        _pad_heads(wq, GLA_HEADS, GLA_DK, GLA_DK_PAD), _pad_heads(wk, GLA_HEADS, GLA_DK, GLA_DK_PAD),
        _pad_heads(wv, GLA_HEADS, GLA_DV, GLA_DV_PAD), _pad_heads(wr, GLA_HEADS, GLA_DV, GLA_DV_PAD),
        wx, jnp.pad(wa, ((0, 0), (0, LANES - GLA_GATE_RANK)))], axis=1).astype(BF16)
    wa2 = jnp.pad(_pad_heads(w_a2, GLA_HEADS, GLA_DK, GLA_DK_PAD), ((0, LANES - GLA_GATE_RANK), (0, 0))).astype(BF16)
    ba = _pad_heads(b_a[None, :], GLA_HEADS, GLA_DK, GLA_DK_PAD)
    br = _pad_heads(b_r[None, :], GLA_HEADS, GLA_DV, GLA_DV_PAD)
    og = jnp.pad(o_gain[None, :], ((0, 0), (0, GLA_DV_PAD - GLA_DV)))
    wo = _pad_heads(w_out[:TOK_WIDTH].T, GLA_HEADS, GLA_DV, GLA_DV_PAD).T.astype(BF16)
    wm = w_out[TOK_WIDTH:].astype(BF16)
    return w, wa2, ba, br, og, wo, wm


def _fox_weights(w_in, b_f, q_gain, k_gain, w_out):
    splits = np.cumsum((TOK_WIDTH, TOK_WIDTH, TOK_WIDTH, FOX_HEADS))
    wq, wk, wv, wf, wx = jnp.split(w_in, [int(s) for s in splits], axis=1)
    padh = lambda a: _pad_heads(a, FOX_HEADS, FOX_HEAD_DIM, LANES)
    w = jnp.concatenate([padh(wq), padh(wk), wv, wx, jnp.pad(wf, ((0, 0), (0, LANES - FOX_HEADS)))],
                        axis=1).astype(BF16)
    bfp = jnp.pad(b_f[None, :], ((0, 0), (0, LANES - FOX_HEADS)))
    qg = padh(jnp.tile(q_gain, FOX_HEADS)[None, :])
    kg = padh(jnp.tile(k_gain, FOX_HEADS)[None, :])
    return w, bfp, qg, kg, w_out[:TOK_WIDTH].astype(BF16), w_out[TOK_WIDTH:].astype(BF16)


TM = 512
GLA_CHUNK_TOKENS = 64
GLA_BLOCK_TOKENS = 256
GLA_DECODE_BATCH = 4
MEM_DECODE_BATCH = 8
FOX_TQ = 1024
FOX_TK = 2048
FOX_PAGES_PER_STEP = 8
CUMSUM_ROWS = 512


def kernel(x_prompt, x_sample, mem_prompt, cache_fox_k, cache_fox_v, cache_fox_logf, page_table, state_gla, cache_mem_k, cache_mem_v, norm_mix, norm_mlp, norm_mem, gla_w_in, gla_w_a2, gla_b_a, gla_b_r, gla_o_norm, gla_w_out, fox_w_in, fox_b_f, fox_q_norm, fox_k_norm, fox_w_out, mem_w_kv, mem_q_norm, mem_k_norm, mlp_w_up, mlp_w_down):
    nb, seq, _ = x_prompt.shape
    ns, tdec, _ = x_sample.shape
    mp, ms = nb * seq, ns * tdec
    yp = x_prompt.reshape(mp, D_MODEL)
    ys = x_sample.reshape(ms, D_MODEL)
    mem2 = mem_prompt.reshape(nb * N_MEM, D_MODEL)
    clt = jnp.pad(jnp.swapaxes(cache_fox_logf, 2, 3), ((0, 0), (0, 0), (0, 2 * SUBLANES - FOX_HEADS), (0, 0)))
    to_t = lambda a: jnp.transpose(a, (0, 1, 3, 4, 2))
    ckt, cvt, cmk, cmv = to_t(cache_fox_k), to_t(cache_fox_v), to_t(cache_mem_k), to_t(cache_mem_v)
    tms = min(TM, ms)

    fk_p, fv_p, fl_p, gs_p, mk_all, mv_all = [], [], [], [], [], []
    fk_s, fv_s, fl_s, gs_s = [], [], [], []
    for l in range(DEPTH):
        i = l // 2
        g_mix = norm_mix[l][None, :]
        mqg = jnp.tile(mem_q_norm[l], MEM_HEADS)[None, :]
        mkg = jnp.tile(mem_k_norm[l], MEM_HEADS)[None, :]
        mk_p, mv_p = _mem_kv(mem2, norm_mem[l][None, :], mem_w_kv[l].astype(BF16), mkg)
        mk_all.append(mk_p.reshape(nb, N_MEM, MEM_HEADS, MEM_HEAD_DIM))
        mv_all.append(mv_p.reshape(nb, N_MEM, MEM_HEADS, MEM_HEAD_DIM))
        mk_p = mk_p.reshape(nb, N_MEM, MEM_WIDTH)
        mv_p = mv_p.reshape(nb, N_MEM, MEM_WIDTH)
        if l % 2 == 0:
            w, wa2, ba, br, og, wo, wm = _gla_weights(gla_w_in[i], gla_w_a2[i], gla_b_a[i], gla_b_r[i],
                                                      gla_o_norm[i], gla_w_out[i])
            outs = []
            for x2, bsz, t, tm, s0, nbat in ((yp, nb, seq, TM, None, nb),
                                             (ys, ns, tdec, tms, state_gla[i], math.gcd(ns, GLA_DECODE_BATCH))):
                q, k, v, r, xq, la = _gla_proj(x2, g_mix, w, wa2, ba, tm)
                r3 = lambda a: a.reshape(bsz, t, a.shape[-1])
                o, s_new = _gla_mix(r3(q), r3(k), r3(la), r3(v), r3(r), og, br, s0,
                                    chunk=GLA_CHUNK_TOKENS if t >= GLA_CHUNK_TOKENS else 2 * SUBLANES,
                                    tblk=GLA_BLOCK_TOKENS, nbatch=nbat)
                outs.append((o.reshape(bsz * t, -1), r3(xq), s_new))
            (o_p, xq_p, sp), (o_s, xq_s, ss) = outs
            gs_p.append(sp)
            gs_s.append(ss)
        else:
            w, bfp, qg, kg, wo, wm = _fox_weights(fox_w_in[i], fox_b_f[i], fox_q_norm[i], fox_k_norm[i], fox_w_out[i])
            unpad = lambda a: a.reshape(a.shape[0], FOX_HEADS, LANES)[:, :, :FOX_HEAD_DIM]
            qa, kp, vf, vt, xq_p, lf = _fox_proj(yp, g_mix, w, qg, kg, bfp, TM)
            r3 = lambda a: a.reshape(nb, seq, a.shape[-1])
            ka = _key_bias(r3(lf), r3(kp), CUMSUM_ROWS)
            o_p = _fox_flash(r3(qa), ka, vt, min(FOX_TQ, seq), min(FOX_TK, seq)).reshape(mp, TOK_WIDTH)
            xq_p = r3(xq_p)
            fk_p.append(unpad(kp).reshape(nb, seq, FOX_HEADS, FOX_HEAD_DIM))
            fv_p.append(vf.reshape(nb, seq, FOX_HEADS, FOX_HEAD_DIM))
            fl_p.append(lf[:, :FOX_HEADS].reshape(nb, seq, FOX_HEADS))
            qa, kp, vf, _, xq_s, lf = _fox_proj(ys, g_mix, w, qg, kg, bfp, tms)
            r3 = lambda a: a.reshape(ns, tdec, a.shape[-1])
            kf = unpad(kp).reshape(ms, TOK_WIDTH)
            o_s = _fox_decode(r3(unpad(qa).astype(F32).reshape(ms, TOK_WIDTH)), r3(kf), r3(vf), r3(lf), ckt, cvt, clt,
                              page_table, i, FOX_PAGES_PER_STEP).reshape(ms, TOK_WIDTH)
            xq_s = r3(xq_s)
            fk_s.append(kf.reshape(ns, tdec, FOX_HEADS, FOX_HEAD_DIM))
            fv_s.append(vf.reshape(ns, tdec, FOX_HEADS, FOX_HEAD_DIM))
            fl_s.append(lf[:, :FOX_HEADS].reshape(ns, tdec, FOX_HEADS))
        xo_p = _mem_attn_prompt(xq_p, mk_p, mv_p, mqg, TM).reshape(mp, MEM_WIDTH)
        xo_s = _mem_attn_decode(xq_s, cmk, cmv, mqg, l, math.gcd(ns, MEM_DECODE_BATCH)).reshape(ms, MEM_WIDTH)
        g_mlp = norm_mlp[l][None, :]
        wu = mlp_w_up[l].astype(BF16)
        wd = mlp_w_down[l].astype(BF16)
        yp = _out_mlp(yp, o_p, xo_p, wo, wm, g_mlp, wu, wd, TM)
        ys = _out_mlp(ys, o_s, xo_s, wo, wm, g_mlp, wu, wd, tms)
    return (yp.reshape(nb, seq, D_MODEL), ys.reshape(ns, tdec, D_MODEL),
            jnp.stack(fk_p, axis=1), jnp.stack(fv_p, axis=1), jnp.stack(fl_p, axis=1),
            jnp.stack(gs_p, axis=0), jnp.stack(mk_all, axis=0), jnp.stack(mv_all, axis=0),
            jnp.stack(fk_s, axis=1), jnp.stack(fv_s, axis=1), jnp.stack(fl_s, axis=1),
            jnp.stack(gs_s, axis=0))
```

```python
import functools
import math

import numpy as np
import jax
import jax.numpy as jnp
from jax import lax
from jax.experimental import pallas as pl
from jax.experimental.pallas import tpu as pltpu

F32 = jnp.float32
BF16 = jnp.bfloat16

D_MODEL = 1024
DEPTH = 4
PAGE_SIZE = 128
TOK_WIDTH = 768
MEM_WIDTH = 256
N_MEM = 256
MEM_HEADS = 4
MEM_HEAD_DIM = 64
GLA_HEADS = 4
GLA_DV = 192
GLA_DK = 96
GLA_GATE_RANK = 16
GLA_TAU = 16.0
FOX_HEAD_DIM = 64
FOX_HEADS = 12
D_FF = 4 * D_MODEL
EPS = 1e-6

LANES = 128
SUBLANES = 8
GLA_DK_PAD = 128
GLA_DV_PAD = 256
NEG_BIG = -1e30
VMEM_LIMIT = 56 * 1024 * 1024


def _cparams(sem):
    return pltpu.CompilerParams(dimension_semantics=sem, vmem_limit_bytes=VMEM_LIMIT)


def _resident(a):
    return pl.BlockSpec(a.shape, lambda i: (0,) * a.ndim, pipeline_mode=pl.Buffered(1))


def _dot(a, b):
    return jnp.dot(a, b, preferred_element_type=F32)


def _dot_nt(a, b):
    return lax.dot_general(a, b, (((1,), (1,)), ((), ())), preferred_element_type=F32)


def _dot_tn(a, b):
    return lax.dot_general(a, b, (((0,), (0,)), ((), ())), preferred_element_type=F32)


def _split3(x):
    hi = x.astype(BF16)
    r1 = x - hi.astype(F32)
    mid = r1.astype(BF16)
    lo = (r1 - mid.astype(F32)).astype(BF16)
    return hi, mid, lo


def _dot01(m01, x):
    hi, mid, lo = _split3(x)
    return (_dot(m01, lo) + _dot(m01, mid)) + _dot(m01, hi)


def _rms(x, g):
    return x * lax.rsqrt(jnp.mean(x * x, axis=-1, keepdims=True) + EPS) * g


def _log_sigmoid(x):
    return jnp.minimum(x, 0.0) - jnp.log(1.0 + jnp.exp(-jnp.abs(x)))


def _half_mask(shape):
    lane = lax.broadcasted_iota(jnp.int32, shape, len(shape) - 1)
    return (lane & FOX_HEAD_DIM) == 0


def _log2(n):
    k = int(n).bit_length() - 1
    assert 1 << k == n, n
    return k


def _ones_where(cond):
    return jnp.where(cond, 1.0, 0.0).astype(BF16)


def _head_rms_pairs(x, gain):
    n = x.shape[-1] // LANES
    outs = []
    for c in range(n):
        xc = x[:, c * LANES:(c + 1) * LANES]
        sq = xc * xc
        even = _half_mask(xc.shape)
        s_e = jnp.sum(jnp.where(even, sq, 0.0), axis=-1, keepdims=True)
        s_o = jnp.sum(jnp.where(even, 0.0, sq), axis=-1, keepdims=True)
        rs = jnp.where(even, lax.rsqrt(s_e / FOX_HEAD_DIM + EPS), lax.rsqrt(s_o / FOX_HEAD_DIM + EPS))
        outs.append(xc * rs)
    return jnp.concatenate(outs, axis=-1) * gain


GLA_COLS = (GLA_HEADS * GLA_DK_PAD, GLA_HEADS * GLA_DK_PAD, GLA_HEADS * GLA_DV_PAD,
            GLA_HEADS * GLA_DV_PAD, MEM_WIDTH, LANES)
GLA_OFF = tuple(int(v) for v in np.cumsum((0,) + GLA_COLS))


def _gla_proj_kernel(x_ref, g_ref, w_ref, wa_ref, ba_ref, q_ref, k_ref, v_ref, r_ref, xq_ref, la_ref):
    xn = _rms(x_ref[...], g_ref[...]).astype(BF16)
    y = _dot(xn, w_ref[...])
    o = GLA_OFF
    q_ref[...] = y[:, o[0]:o[1]] * (GLA_DK ** -0.5)
    k_ref[...] = y[:, o[1]:o[2]]
    v_ref[...] = y[:, o[2]:o[3]].astype(v_ref.dtype)
    r_ref[...] = y[:, o[3]:o[4]]
    xq_ref[...] = y[:, o[4]:o[5]]
    a = _dot(y[:, o[5]:o[6]].astype(BF16), wa_ref[...]) + ba_ref[...]
    la_ref[...] = _log_sigmoid(a) * (1.0 / GLA_TAU)


def _gla_proj(x, g, w, wa, ba, tm):
    m = x.shape[0]
    row = lambda n: pl.BlockSpec((tm, n), lambda i: (i, 0))
    full = _resident
    hk, hv = GLA_HEADS * GLA_DK_PAD, GLA_HEADS * GLA_DV_PAD
    out_shape = (jax.ShapeDtypeStruct((m, hk), F32), jax.ShapeDtypeStruct((m, hk), F32),
                 jax.ShapeDtypeStruct((m, hv), F32), jax.ShapeDtypeStruct((m, hv), F32),
                 jax.ShapeDtypeStruct((m, MEM_WIDTH), F32), jax.ShapeDtypeStruct((m, hk), F32))
    return pl.pallas_call(
        _gla_proj_kernel,
        grid=(m // tm,),
        in_specs=[row(D_MODEL), full(g), full(w), full(wa), full(ba)],
        out_specs=(row(hk), row(hk), row(hv), row(hv), row(MEM_WIDTH), row(hk)),
        out_shape=out_shape,
        compiler_params=_cparams(("parallel",)),
    )(x, g, w, wa, ba)


FOX_PAD_WIDTH = FOX_HEADS * LANES
FOX_AUX = 3
LOG2E = 1.4426950408889634
FOX_COLS = (FOX_PAD_WIDTH, FOX_PAD_WIDTH, TOK_WIDTH, MEM_WIDTH, LANES)
FOX_OFF = tuple(int(v) for v in np.cumsum((0,) + FOX_COLS))


def _head_rms_padded(x, gain):
    outs = []
    for h in range(x.shape[-1] // LANES):
        xh = x[:, h * LANES:(h + 1) * LANES]
        ms = jnp.sum(xh * xh, axis=-1, keepdims=True) * (1.0 / FOX_HEAD_DIM)
        outs.append(xh * lax.rsqrt(ms + EPS))
    return jnp.concatenate(outs, axis=-1) * gain


def _fox_proj_kernel(x_ref, g_ref, w_ref, qg_ref, kg_ref, bf_ref, qa_ref, kp_ref, vf_ref, vt_ref, xq_ref, lf_ref):
    xn = _rms(x_ref[...], g_ref[...]).astype(BF16)
    y = _dot(xn, w_ref[...])
    o = FOX_OFF
    q = _head_rms_padded(y[:, o[0]:o[1]], qg_ref[...]) * (FOX_HEAD_DIM ** -0.5 * LOG2E)
    lane = lax.broadcasted_iota(jnp.int32, q.shape, 1) & (LANES - 1)
    aux = jnp.logical_and(lane >= FOX_HEAD_DIM, lane < FOX_HEAD_DIM + FOX_AUX)
    qa_ref[...] = jnp.where(aux, 1.0, q).astype(BF16)
    kp_ref[...] = _head_rms_padded(y[:, o[1]:o[2]], kg_ref[...])
    v = y[:, o[2]:o[3]]
    vf_ref[...] = v
    vt_ref[...] = v.T.astype(BF16)
    xq_ref[...] = y[:, o[3]:o[4]]
    lf_ref[...] = _log_sigmoid(y[:, o[4]:o[5]] + bf_ref[...])


def _fox_proj(x, g, w, qg, kg, bfp, tm):
    m = x.shape[0]
    row = lambda n: pl.BlockSpec((tm, n), lambda i: (i, 0))
    full = _resident
    tw, pw = TOK_WIDTH, FOX_PAD_WIDTH
    out_shape = (jax.ShapeDtypeStruct((m, pw), BF16), jax.ShapeDtypeStruct((m, pw), F32),
                 jax.ShapeDtypeStruct((m, tw), F32), jax.ShapeDtypeStruct((tw, m), BF16),
                 jax.ShapeDtypeStruct((m, MEM_WIDTH), F32), jax.ShapeDtypeStruct((m, LANES), F32))
    return pl.pallas_call(
        _fox_proj_kernel,
        grid=(m // tm,),
        in_specs=[row(D_MODEL), full(g), full(w), full(qg), full(kg), full(bfp)],
        out_specs=(row(pw), row(pw), row(tw), pl.BlockSpec((tw, tm), lambda i: (0, i)), row(MEM_WIDTH), row(LANES)),
        out_shape=out_shape,
        compiler_params=_cparams(("parallel",)),
    )(x, g, w, qg, kg, bfp)


def _mem_kv_kernel(x_ref, g_ref, w_ref, kg_ref, k_ref, v_ref):
    xn = _rms(x_ref[...], g_ref[...]).astype(BF16)
    y = _dot(xn, w_ref[...])
    k_ref[...] = _head_rms_pairs(y[:, :MEM_WIDTH], kg_ref[...])
    v_ref[...] = y[:, MEM_WIDTH:]


def _mem_kv(x, g, w, kg):
    m = x.shape[0]
    full = _resident
    out = jax.ShapeDtypeStruct((m, MEM_WIDTH), F32)
    return pl.pallas_call(
        _mem_kv_kernel,
        grid=(1,),
        in_specs=[full(x), full(g), full(w), full(kg)],
        out_specs=(pl.BlockSpec((m, MEM_WIDTH), lambda i: (0, 0)),) * 2,
        out_shape=(out, out),
        compiler_params=_cparams(("arbitrary",)),
    )(x, g, w, kg)


def _gla_levels(c):
    out, h = [], c // 2
    while h >= SUBLANES:
        out.append(h)
        h //= 2
    return out


def _gla_cum_matrix(c):
    t = np.arange(c)
    blocks = [(t[None, :] <= t[:, None])]
    for h in _gla_levels(c):
        anchor = (t // (2 * h)) * (2 * h) + h - 1
        blocks.append(t[None, :] <= anchor[:, None])
    return jnp.asarray(np.concatenate(blocks, axis=0).astype(np.float32), dtype=BF16)


def _gla_kernel(*refs, chunk, n_sub, t_valid, has_s0, nbatch):
    if has_s0:
        cum_ref, q_ref, k_ref, la_ref, v_ref, r_ref, og_ref, br_ref, s0_ref, o_ref, sout_ref, st_ref, pad_ref = refs
    else:
        cum_ref, q_ref, k_ref, la_ref, v_ref, r_ref, og_ref, br_ref, o_ref, sout_ref, st_ref, pad_ref = refs
        s0_ref = None
    c = chunk
    levels = _gla_levels(c)
    nb = c // SUBLANES
    step = pl.program_id(1)
    chains = [(bi, h) for bi in range(nbatch) for h in range(GLA_HEADS)]

    @pl.when(step == 0)
    def _():
        for ci, (bi, h) in enumerate(chains):
            if has_s0:
                pad_ref[...] = jnp.zeros(pad_ref.shape, F32)
                pad_ref[:GLA_DK, :GLA_DV] = s0_ref[bi, h]
                st_ref[ci] = pad_ref[...].T
            else:
                st_ref[ci] = jnp.zeros(st_ref.shape[1:], F32)

    def load(ref, bi, rows, h, width):
        x = ref[bi, rows, h * width:(h + 1) * width]
        if t_valid < c:
            x = jnp.concatenate([x, jnp.zeros((c - t_valid, width), F32)], axis=0)
        return x

    def do_chunk(ci, bi, h, rows):
        q = load(q_ref, bi, rows, h, GLA_DK_PAD)
        k = load(k_ref, bi, rows, h, GLA_DK_PAD)
        la = load(la_ref, bi, rows, h, GLA_DK_PAD)
        v = load(v_ref, bi, rows, h, GLA_DV_PAD)
        vb = v.astype(BF16)
        cum = _dot01(cum_ref[...], la)
        b = cum[:c]
        b_last = b[c - 1:c, :]
        st = st_ref[ci]
        o = _dot_nt((q * jnp.exp(b)).astype(BF16), st.astype(BF16))
        if levels:
            ti = lax.broadcasted_iota(jnp.int32, (c, c), 0)
            si = lax.broadcasted_iota(jnp.int32, (c, c), 1)
            row = lax.broadcasted_iota(jnp.int32, (c, GLA_DK_PAD), 0)
            attn = jnp.zeros((c, c), F32)
            for li, half in enumerate(levels):
                anchor = cum[(li + 1) * c:(li + 2) * c]
                gdec = jnp.exp(-jnp.abs(b - anchor))
                first = (row & half) == 0
                ql = jnp.where(first, 0.0, q * gdec).astype(BF16)
                kl = jnp.where(first, k * gdec, 0.0).astype(BF16)
                a = _dot_nt(ql, kl)
                sh = _log2(2 * half)
                attn = attn + jnp.where((ti >> sh) == (si >> sh), a, 0.0)
        else:
            ti = lax.broadcasted_iota(jnp.int32, (c, c), 0)
            si = lax.broadcasted_iota(jnp.int32, (c, c), 1)
            attn = jnp.zeros((c, c), F32)
        q3 = q.reshape(nb, SUBLANES, GLA_DK_PAD)
        k3 = k.reshape(nb, SUBLANES, GLA_DK_PAD)
        b3 = b.reshape(nb, SUBLANES, GLA_DK_PAD)
        sub = lax.broadcasted_iota(jnp.int32, (nb, SUBLANES, GLA_DK_PAD), 1)
        rel = si - (ti & -SUBLANES)
        for j in range(SUBLANES):
            kj = jnp.broadcast_to(k3[:, j:j + 1, :], k3.shape)
            bj = jnp.broadcast_to(b3[:, j:j + 1, :], b3.shape)
            e = jnp.exp(jnp.where(sub >= j, b3 - bj, -jnp.inf))
            col = jnp.sum(q3 * kj * e, axis=-1, keepdims=True).reshape(c, 1)
            attn = jnp.where(rel == j, col, attn)
        o = o + _dot(attn.astype(BF16), vb)
        kg = (k * jnp.exp(b_last - b)).astype(BF16)
        st_new = st * jnp.exp(b_last) + _dot_tn(vb, kg)
        ms = jnp.sum(o * o, axis=-1, keepdims=True) * (1.0 / GLA_DV)
        on = o * lax.rsqrt(ms + EPS) * og_ref[...]
        r = load(r_ref, bi, rows, h, GLA_DV_PAD) + br_ref[:, h * GLA_DV_PAD:(h + 1) * GLA_DV_PAD]
        res = on * (r * (1.0 / (1.0 + jnp.exp(-r))))
        return (res[:t_valid] if t_valid < c else res), st_new

    def all_chains(rows):
        outs = [do_chunk(ci, bi, h, rows) for ci, (bi, h) in enumerate(chains)]
        for ci, (bi, h) in enumerate(chains):
            res, st_new = outs[ci]
            o_ref[bi, rows, h * GLA_DV_PAD:(h + 1) * GLA_DV_PAD] = res.astype(o_ref.dtype)
            st_ref[ci] = st_new

    if n_sub == 1:
        all_chains(slice(None))
    else:
        def body(i, carry):
            all_chains(pl.ds(pl.multiple_of(i * c, c), c))
            return carry
        lax.fori_loop(0, n_sub, body, 0)

    @pl.when(step == pl.num_programs(1) - 1)
    def _():
        for ci, (bi, h) in enumerate(chains):
            pad_ref[...] = st_ref[ci].T
            sout_ref[bi, h] = pad_ref[:GLA_DK, :GLA_DV]


def _gla_mix(q, k, la, v, r, og, br, s0, *, chunk, tblk, nbatch):
    bsz, t, _ = q.shape
    has_s0 = s0 is not None
    if t < chunk:
        t_valid, n_sub, tblk = t, 1, t
    else:
        t_valid, n_sub = chunk, tblk // chunk
    cum = _gla_cum_matrix(chunk)
    hk, hv = GLA_HEADS * GLA_DK_PAD, GLA_HEADS * GLA_DV_PAD
    kblk = pl.BlockSpec((nbatch, tblk, hk), lambda b, i: (b, i, 0))
    vblk = pl.BlockSpec((nbatch, tblk, hv), lambda b, i: (b, i, 0))
    sblk = pl.BlockSpec((nbatch, GLA_HEADS, GLA_DK, GLA_DV), lambda b, i: (b, 0, 0, 0))
    in_specs = [pl.BlockSpec(cum.shape, lambda b, i: (0, 0)), kblk, kblk, kblk, vblk, vblk,
                pl.BlockSpec((1, GLA_DV_PAD), lambda b, i: (0, 0)),
                pl.BlockSpec((1, hv), lambda b, i: (0, 0))]
    args = [cum, q, k, la, v, r, og, br]
    if has_s0:
        in_specs.append(sblk)
        args.append(s0)
    kern = functools.partial(_gla_kernel, chunk=chunk, n_sub=n_sub, t_valid=t_valid, has_s0=has_s0, nbatch=nbatch)
    return pl.pallas_call(
        kern,
        grid=(bsz // nbatch, t // tblk),
        in_specs=in_specs,
        out_specs=(vblk, sblk),
        out_shape=(jax.ShapeDtypeStruct((bsz, t, hv), BF16 if t >= chunk else F32),
                   jax.ShapeDtypeStruct((bsz, GLA_HEADS, GLA_DK, GLA_DV), F32)),
        scratch_shapes=[pltpu.VMEM((nbatch * GLA_HEADS, GLA_DV_PAD, GLA_DK_PAD), F32),
                        pltpu.VMEM((GLA_DK_PAD, GLA_DV_PAD), F32)],
        compiler_params=_cparams(("parallel", "arbitrary")),
    )(*args)


def _key_bias_kernel(tri_ref, lf_ref, kp_ref, ka_ref, carry_ref):
    @pl.when(pl.program_id(1) == 0)
    def _():
        carry_ref[...] = jnp.zeros(carry_ref.shape, F32)
    c = _dot01(tri_ref[...], lf_ref[0]) + carry_ref[...]
    carry_ref[...] = c[c.shape[0] - 1:, :]
    nbias = c * (-LOG2E)
    hlane = lax.broadcasted_iota(jnp.int32, c.shape, 1)
    lane = lax.broadcasted_iota(jnp.int32, c.shape, 1)
    for h in range(FOX_HEADS):
        col = jnp.sum(jnp.where(hlane == h, nbias, 0.0), axis=-1, keepdims=True)
        hi = col.astype(BF16).astype(F32)
        mid = (col - hi).astype(BF16).astype(F32)
        lo = col - hi - mid
        kh = kp_ref[0, :, h * LANES:(h + 1) * LANES]
        kh = jnp.where(lane == FOX_HEAD_DIM, hi, kh)
        kh = jnp.where(lane == FOX_HEAD_DIM + 1, mid, kh)
        kh = jnp.where(lane == FOX_HEAD_DIM + 2, lo, kh)
        ka_ref[0, :, h * LANES:(h + 1) * LANES] = kh.astype(BF16)


def _key_bias(lf, kp, tc):
    bsz, t, n = lf.shape
    tri = jnp.asarray(np.tril(np.ones((tc, tc), np.float32)), dtype=BF16)
    kblk = pl.BlockSpec((1, tc, FOX_PAD_WIDTH), lambda b, i: (b, i, 0))
    return pl.pallas_call(
        _key_bias_kernel,
        grid=(bsz, t // tc),
        in_specs=[pl.BlockSpec((tc, tc), lambda b, i: (0, 0)), pl.BlockSpec((1, tc, n), lambda b, i: (b, i, 0)), kblk],
        out_specs=kblk,
        out_shape=jax.ShapeDtypeStruct((bsz, t, FOX_PAD_WIDTH), BF16),
        scratch_shapes=[pltpu.VMEM((1, n), F32)],
        compiler_params=_cparams(("parallel", "arbitrary")),
    )(tri, lf, kp)


def _fox_flash_kernel(qi_ref, kj_ref, q_ref, k_ref, vt_ref, o_ref, m_ref, l_ref, acc_ref, *, tq, tk):
    qi = qi_ref[pl.program_id(2)]
    kj = kj_ref[pl.program_id(2)]
    last = ((qi + 1) * tq - 1) // tk

    @pl.when(kj == 0)
    def _():
        m_ref[...] = jnp.full(m_ref.shape, NEG_BIG, F32)
        l_ref[...] = jnp.zeros(l_ref.shape, F32)
        acc_ref[...] = jnp.zeros(acc_ref.shape, F32)

    def update(masked, nk):
        vt = vt_ref[:, :nk]
        if masked:
            keyi = kj * tk + lax.broadcasted_iota(jnp.int32, (nk, tq), 0)
            qryi = qi * tq + lax.broadcasted_iota(jnp.int32, (nk, tq), 1)
            keep = keyi <= qryi
        for hh in range(2):
            st = _dot_nt(k_ref[0, :nk, hh * LANES:(hh + 1) * LANES], q_ref[0, :, hh * LANES:(hh + 1) * LANES])
            if masked:
                st = jnp.where(keep, st, NEG_BIG)
            m_old = m_ref[hh]
            m_new = jnp.maximum(m_old, jnp.max(st, axis=0, keepdims=True))
            alpha = jnp.exp2(m_old - m_new)
            p = jnp.exp2(st - m_new)
            l_ref[hh] = alpha * l_ref[hh] + jnp.sum(p, axis=0, keepdims=True)
            m_ref[hh] = m_new
            rows = slice(hh * FOX_HEAD_DIM, (hh + 1) * FOX_HEAD_DIM)
            acc_ref[rows, :] = alpha * acc_ref[rows, :] + _dot(vt, p.astype(BF16))[rows, :]

    crosses = (kj + 1) * tk - 1 > qi * tq
    visible = (qi + 1) * tq - kj * tk
    if tk % tq == 0:
        for nk in range(tq, tk + 1, tq):
            @pl.when(jnp.logical_and(crosses, visible == nk))
            def _(nk=nk):
                update(True, nk)
    else:
        @pl.when(crosses)
        def _():
            update(True, tk)

    @pl.when(jnp.logical_not(crosses))
    def _():
        update(False, tk)

    @pl.when(kj == last)
    def _():
        hd = FOX_HEAD_DIM
        ot = jnp.concatenate([acc_ref[:hd, :] / l_ref[0], acc_ref[hd:, :] / l_ref[1]], axis=0)
        o_ref[0] = ot.T.astype(o_ref.dtype)


def _fox_flash(q, k, vt, tq, tk):
    bsz, t, _ = q.shape
    npair = FOX_HEADS // 2
    pairs = [(i, j) for i in range(t // tq) for j in range(((i + 1) * tq - 1) // tk + 1)]
    qi_tbl = jnp.asarray([i for i, _ in pairs], jnp.int32)
    kj_tbl = jnp.asarray([j for _, j in pairs], jnp.int32)
    qblk = pl.BlockSpec((1, tq, 2 * LANES), lambda b, p, s, qi, kj: (b, qi[s], p))
    kblk = pl.BlockSpec((1, tk, 2 * LANES), lambda b, p, s, qi, kj: (b, kj[s], p))
    grid_spec = pltpu.PrefetchScalarGridSpec(
        num_scalar_prefetch=2,
        grid=(bsz, npair, len(pairs)),
        in_specs=[qblk, kblk, pl.BlockSpec((LANES, tk), lambda b, p, s, qi, kj: (p, b * (t // tk) + kj[s]))],
        out_specs=pl.BlockSpec((1, tq, LANES), lambda b, p, s, qi, kj: (b, qi[s], p)),
        scratch_shapes=[pltpu.VMEM((2, 1, tq), F32), pltpu.VMEM((2, 1, tq), F32), pltpu.VMEM((LANES, tq), F32)],
    )
    return pl.pallas_call(
        functools.partial(_fox_flash_kernel, tq=tq, tk=tk),
        grid_spec=grid_spec,
        out_shape=jax.ShapeDtypeStruct((bsz, t, TOK_WIDTH), BF16),
        compiler_params=_cparams(("parallel", "parallel", "arbitrary")),
    )(qi_tbl, kj_tbl, q, k, vt)


def _head_rows(x, heads):
    lane = lax.broadcasted_iota(jnp.int32, x.shape, 1) >> _log2(FOX_HEAD_DIM)
    zero = jnp.zeros(x.shape, x.dtype)
    return jnp.concatenate([jnp.where(lane == h, x, zero) for h in range(heads)], axis=0)


def _head_diag(z, heads, t):
    lane = lax.broadcasted_iota(jnp.int32, (t, z.shape[1]), 1) >> _log2(FOX_HEAD_DIM)
    out = jnp.zeros((t, z.shape[1]), z.dtype)
    for h in range(heads):
        out = jnp.where(lane == h, z[h * t:(h + 1) * t, :], out)
    return out


def _pad_rows(x, n):
    return jnp.concatenate([x, jnp.zeros((n - x.shape[0], x.shape[1]), x.dtype)], axis=0)


def _fox_decode_kernel(pt_ref, q_ref, kn_ref, vn_ref, lfn_ref, *rest, t_new, pps):
    del pt_ref
    kc_refs, vc_refs, lc_refs = rest[:pps], rest[pps:2 * pps], rest[2 * pps:3 * pps]
    o_ref, wq_ref, m_ref, l_ref, acc_ref, carry_ref, cnk_ref = rest[3 * pps:]
    p = pl.program_id(1)
    npg = pl.num_programs(1)
    heads = FOX_HEADS
    rows = heads * t_new
    tpad = 2 * SUBLANES

    @pl.when(p == 0)
    def _():
        wq_ref[...] = _head_rows(q_ref[0], heads).astype(BF16)
        m_ref[...] = jnp.full(m_ref.shape, NEG_BIG, F32)
        l_ref[...] = jnp.zeros(l_ref.shape, F32)
        acc_ref[...] = jnp.zeros(acc_ref.shape, F32)
        carry_ref[...] = jnp.zeros(carry_ref.shape, F32)
        lfn = _pad_rows(lfn_ref[0] * LOG2E, tpad)
        ti = lax.broadcasted_iota(jnp.int32, (tpad, tpad), 0)
        si = lax.broadcasted_iota(jnp.int32, (tpad, tpad), 1)
        cs = _dot01(_ones_where(si <= ti), lfn)
        rh = lax.broadcasted_iota(jnp.int32, (rows, LANES), 0) >> _log2(t_new)
        lh = lax.broadcasted_iota(jnp.int32, (rows, LANES), 1)
        hsel = _ones_where(rh == lh)
        hi, mid, lo = _split3(cs)
        cnk_ref[...] = (_dot_nt(hsel, lo) + _dot_nt(hsel, mid)) + _dot_nt(hsel, hi)

    def online(s, pv_of):
        m_old = m_ref[...]
        m_new = jnp.maximum(m_old, jnp.max(s, axis=-1, keepdims=True))
        alpha = jnp.exp2(m_old - m_new)
        pr = jnp.exp2(s - m_new)
        l_ref[...] = alpha * l_ref[...] + jnp.sum(pr, axis=-1, keepdims=True)
        m_ref[...] = m_new
        acc_ref[...] = alpha * acc_ref[...] + pv_of(pr.astype(BF16))

    cnk = cnk_ref[...]
    ri = lax.broadcasted_iota(jnp.int32, cnk.shape, 0) & (t_new - 1)
    ci = lax.broadcasted_iota(jnp.int32, cnk.shape, 1)
    cn_col = jnp.sum(jnp.where(ri == ci, cnk, 0.0), axis=-1, keepdims=True)

    ji = lax.broadcasted_iota(jnp.int32, (PAGE_SIZE, PAGE_SIZE), 0)
    si2 = lax.broadcasted_iota(jnp.int32, (PAGE_SIZE, PAGE_SIZE), 1)
    after = _ones_where(ji > si2)
    carry = carry_ref[...]
    parts = []
    for j in range(pps):
        lp = lc_refs[j][0, 0] * LOG2E
        suffix = _dot01_rhs(lp, after) + carry
        carry = carry + jnp.sum(lp, axis=-1, keepdims=True)
        r96 = jnp.concatenate([jnp.broadcast_to(suffix[h:h + 1, :], (t_new, PAGE_SIZE)) for h in range(heads)],
                              axis=0)
        kt = kc_refs[j][0, 0].reshape(TOK_WIDTH, PAGE_SIZE).astype(BF16)
        parts.append(_dot(wq_ref[...], kt) + r96)
    carry_ref[...] = carry

    def pv_pages(pr):
        out = None
        for j in range(pps):
            vt = vc_refs[j][0, 0].reshape(TOK_WIDTH, PAGE_SIZE).astype(BF16)
            pvj = _dot_nt(pr[:, j * PAGE_SIZE:(j + 1) * PAGE_SIZE], vt)
            out = pvj if out is None else out + pvj
        return out

    online(jnp.concatenate(parts, axis=1) + cn_col, pv_pages)

    @pl.when(p == npg - 1)
    def _():
        kn = _pad_rows(kn_ref[0], tpad).astype(BF16)
        vn = _pad_rows(vn_ref[0], tpad).astype(BF16)
        sn = jnp.where(ci <= ri, _dot_nt(wq_ref[...], kn) + cn_col - cnk, NEG_BIG)
        online(sn, lambda pr: _dot(pr, vn))
        o_ref[0] = _head_diag(acc_ref[...] / l_ref[...], heads, t_new).astype(o_ref.dtype)


def _dot01_rhs(x, m01):
    hi, mid, lo = _split3(x)
    return (_dot(lo, m01) + _dot(mid, m01)) + _dot(hi, m01)


def _fox_decode(q, kn, vn, lfn, cache_k, cache_v, cache_lt, page_table, layer, pps):
    nseq, t_new, _ = q.shape
    heads, hd = cache_k.shape[2], cache_k.shape[3]
    npg = page_table.shape[1]
    pps = math.gcd(npg, pps)
    rows = heads * t_new
    tpad = 2 * SUBLANES
    seq = lambda n: pl.BlockSpec((1, t_new, n), lambda b, p, pt: (b, 0, 0))

    def page_idx(j, ndim):
        return lambda b, p, pt: (pt[b, npg - 1 - (p * pps + j)], layer) + (0,) * (ndim - 2)

    kv_specs = [pl.BlockSpec((1, 1, heads, hd, PAGE_SIZE), page_idx(j, 5)) for j in range(pps)]
    lt_specs = [pl.BlockSpec((1, 1, tpad, PAGE_SIZE), page_idx(j, 4)) for j in range(pps)]
    grid_spec = pltpu.PrefetchScalarGridSpec(
        num_scalar_prefetch=1,
        grid=(nseq, npg // pps),
        in_specs=[seq(TOK_WIDTH), seq(TOK_WIDTH), seq(TOK_WIDTH), seq(LANES)] + kv_specs + kv_specs + lt_specs,
        out_specs=seq(TOK_WIDTH),
        scratch_shapes=[pltpu.VMEM((rows, TOK_WIDTH), BF16), pltpu.VMEM((rows, 1), F32),
                        pltpu.VMEM((rows, 1), F32), pltpu.VMEM((rows, TOK_WIDTH), F32),
                        pltpu.VMEM((tpad, 1), F32), pltpu.VMEM((rows, tpad), F32)],
    )
    return pl.pallas_call(
        functools.partial(_fox_decode_kernel, t_new=t_new, pps=pps),
        grid_spec=grid_spec,
        out_shape=jax.ShapeDtypeStruct((nseq, t_new, TOK_WIDTH), F32),
        compiler_params=_cparams(("parallel", "arbitrary")),
    )(page_table, q, kn, vn, lfn, *([cache_k] * pps), *([cache_v] * pps), *([cache_lt] * pps))


def _mem_attn_prompt_kernel(xq_ref, mk_ref, mv_ref, g_ref, o_ref):
    q = _head_rms_pairs(xq_ref[0], g_ref[...]) * (MEM_HEAD_DIM ** -0.5)
    outs = []
    for pr in range(MEM_WIDTH // LANES):
        cols = slice(pr * LANES, (pr + 1) * LANES)
        qp = q[:, cols].astype(BF16)
        kp = mk_ref[0][:, cols].astype(BF16)
        vp = mv_ref[0][:, cols].astype(BF16)
        even = _half_mask(qp.shape)
        zero = jnp.zeros(qp.shape, BF16)
        res = []
        for qh in (jnp.where(even, qp, zero), jnp.where(even, zero, qp)):
            s = _dot_nt(qh, kp)
            e = jnp.exp(s - jnp.max(s, axis=-1, keepdims=True))
            pv = _dot(e.astype(BF16), vp)
            res.append(pv / jnp.sum(e, axis=-1, keepdims=True))
        outs.append(jnp.where(even, res[0], res[1]))
    o_ref[0] = jnp.concatenate(outs, axis=-1).astype(o_ref.dtype)


def _mem_attn_prompt(xq, mk, mv, g, tq):
    bsz, t, _ = xq.shape
    qblk = pl.BlockSpec((1, tq, MEM_WIDTH), lambda b, i: (b, i, 0))
    mblk = pl.BlockSpec((1, N_MEM, MEM_WIDTH), lambda b, i: (b, 0, 0))
    return pl.pallas_call(
        _mem_attn_prompt_kernel,
        grid=(bsz, t // tq),
        in_specs=[qblk, mblk, mblk, pl.BlockSpec((1, MEM_WIDTH), lambda b, i: (0, 0))],
        out_specs=qblk,
        out_shape=jax.ShapeDtypeStruct((bsz, t, MEM_WIDTH), BF16),
        compiler_params=_cparams(("parallel", "parallel")),
    )(xq, mk, mv, g)


def _mem_attn_decode_kernel(xq_ref, mk_ref, mv_ref, g_ref, o_ref, *, t_new, nbatch):
    for bi in range(nbatch):
        q = _head_rms_pairs(xq_ref[bi], g_ref[...]) * (MEM_HEAD_DIM ** -0.5)
        wq = _head_rows(q, MEM_HEADS).astype(BF16)
        kt = mk_ref[0, bi].reshape(MEM_WIDTH, N_MEM).astype(BF16)
        vt = mv_ref[0, bi].reshape(MEM_WIDTH, N_MEM).astype(BF16)
        s = _dot(wq, kt)
        e = jnp.exp(s - jnp.max(s, axis=-1, keepdims=True))
        z = _dot_nt(e.astype(BF16), vt) / jnp.sum(e, axis=-1, keepdims=True)
        o_ref[bi] = _head_diag(z, MEM_HEADS, t_new).astype(o_ref.dtype)


def _mem_attn_decode(xq, cache_mk, cache_mv, g, layer, nbatch):
    nseq, t_new, _ = xq.shape
    qblk = pl.BlockSpec((nbatch, t_new, MEM_WIDTH), lambda b: (b, 0, 0))
    mblk = pl.BlockSpec((1, nbatch, MEM_HEADS, MEM_HEAD_DIM, N_MEM), lambda b: (layer, b, 0, 0, 0))
    return pl.pallas_call(
        functools.partial(_mem_attn_decode_kernel, t_new=t_new, nbatch=nbatch),
        grid=(nseq // nbatch,),
        in_specs=[qblk, mblk, mblk, pl.BlockSpec((1, MEM_WIDTH), lambda b: (0, 0))],
        out_specs=qblk,
        out_shape=jax.ShapeDtypeStruct((nseq, t_new, MEM_WIDTH), F32),
        compiler_params=_cparams(("parallel",)),
    )(xq, cache_mk, cache_mv, g)


FF_CHUNK = 1024


def _out_mlp_kernel(x_ref, o_ref, xo_ref, wo_ref, wm_ref, g_ref, wu_ref, wd_ref, y_ref):
    y_ref[...] = (x_ref[...] + _dot(o_ref[...].astype(BF16), wo_ref[...])
                  + _dot(xo_ref[...].astype(BF16), wm_ref[...]))
    y1 = y_ref[...]
    xn = _rms(y1, g_ref[...]).astype(BF16)
    acc = y1
    for c in range(D_FF // FF_CHUNK):
        cols = slice(c * FF_CHUNK, (c + 1) * FF_CHUNK)
        h = jnp.maximum(_dot(xn, wu_ref[:, cols]), 0.0)
        acc = acc + _dot((h * h).astype(BF16), wd_ref[cols, :])
    y_ref[...] = acc


def _out_mlp(x, o, xo, wo, wm, g, wu, wd, tm):
    m = x.shape[0]
    row = lambda n: pl.BlockSpec((tm, n), lambda i: (i, 0))
    full = lambda a: pl.BlockSpec(a.shape, lambda i: (0,) * a.ndim, pipeline_mode=pl.Buffered(1))
    return pl.pallas_call(
        _out_mlp_kernel,
        grid=(m // tm,),
        in_specs=[row(D_MODEL), row(o.shape[1]), row(MEM_WIDTH), full(wo), full(wm), full(g), full(wu), full(wd)],
        out_specs=row(D_MODEL),
        out_shape=jax.ShapeDtypeStruct((m, D_MODEL), F32),
        compiler_params=_cparams(("parallel",)),
    )(x, o, xo, wo, wm, g, wu, wd)


def _pad_heads(w, heads, d, dpad):
    lead = w.shape[:-1]
    w = w.reshape(lead + (heads, d))
    w = jnp.pad(w, [(0, 0)] * len(lead) + [(0, 0), (0, dpad - d)])
    return w.reshape(lead + (heads * dpad,))


def _gla_weights(w_in, w_a2, b_a, b_r, o_gain, w_out):
    hk = GLA_HEADS * GLA_DK
    splits = np.cumsum((hk, hk, TOK_WIDTH, GLA_GATE_RANK, TOK_WIDTH))
    wq, wk, wv, wa, wr, wx = jnp.split(w_in, [int(s) for s in splits], axis=1)
    w = jnp.concatenate([
        _pad_heads(wq, GLA_HEADS, GLA_DK, GLA_DK_PAD), _pad_heads(wk, GLA_HEADS, GLA_DK, GLA_DK_PAD),
        _pad_heads(wv, GLA_HEADS, GLA_DV, GLA_DV_PAD), _pad_heads(wr, GLA_HEADS, GLA_DV, GLA_DV_PAD),
        wx, jnp.pad(wa, ((0, 0), (0, LANES - GLA_GATE_RANK)))], axis=1).astype(BF16)
    wa2 = jnp.pad(_pad_heads(w_a2, GLA_HEADS, GLA_DK, GLA_DK_PAD), ((0, LANES - GLA_GATE_RANK), (0, 0))).astype(BF16)
    ba = _pad_heads(b_a[None, :], GLA_HEADS, GLA_DK, GLA_DK_PAD)
    br = _pad_heads(b_r[None, :], GLA_HEADS, GLA_DV, GLA_DV_PAD)
    og = jnp.pad(o_gain[None, :], ((0, 0), (0, GLA_DV_PAD - GLA_DV)))
    wo = _pad_heads(w_out[:TOK_WIDTH].T, GLA_HEADS, GLA_DV, GLA_DV_PAD).T.astype(BF16)
    wm = w_out[TOK_WIDTH:].astype(BF16)
    return w, wa2, ba, br, og, wo, wm


def _fox_weights(w_in, b_f, q_gain, k_gain, w_out):
    splits = np.cumsum((TOK_WIDTH, TOK_WIDTH, TOK_WIDTH, FOX_HEADS))
    wq, wk, wv, wf, wx = jnp.split(w_in, [int(s) for s in splits], axis=1)
    padh = lambda a: _pad_heads(a, FOX_HEADS, FOX_HEAD_DIM, LANES)
    w = jnp.concatenate([padh(wq), padh(wk), wv, wx, jnp.pad(wf, ((0, 0), (0, LANES - FOX_HEADS)))],
                        axis=1).astype(BF16)
    bfp = jnp.pad(b_f[None, :], ((0, 0), (0, LANES - FOX_HEADS)))
    qg = padh(jnp.tile(q_gain, FOX_HEADS)[None, :])
    kg = padh(jnp.tile(k_gain, FOX_HEADS)[None, :])
    return w, bfp, qg, kg, w_out[:TOK_WIDTH].astype(BF16), w_out[TOK_WIDTH:].astype(BF16)


TM = 512
GLA_CHUNK_TOKENS = 128
GLA_BLOCK_TOKENS = 256
GLA_DECODE_BATCH = 4
MEM_DECODE_BATCH = 8
FOX_TQ = 1024
FOX_TK = 2048
FOX_PAGES_PER_STEP = 8
CUMSUM_ROWS = 512


def kernel(x_prompt, x_sample, mem_prompt, cache_fox_k, cache_fox_v, cache_fox_logf, page_table, state_gla, cache_mem_k, cache_mem_v, norm_mix, norm_mlp, norm_mem, gla_w_in, gla_w_a2, gla_b_a, gla_b_r, gla_o_norm, gla_w_out, fox_w_in, fox_b_f, fox_q_norm, fox_k_norm, fox_w_out, mem_w_kv, mem_q_norm, mem_k_norm, mlp_w_up, mlp_w_down):
    nb, seq, _ = x_prompt.shape
    ns, tdec, _ = x_sample.shape
    mp, ms = nb * seq, ns * tdec
    yp = x_prompt.reshape(mp, D_MODEL)
    ys = x_sample.reshape(ms, D_MODEL)
    mem2 = mem_prompt.reshape(nb * N_MEM, D_MODEL)
    clt = jnp.pad(jnp.swapaxes(cache_fox_logf, 2, 3), ((0, 0), (0, 0), (0, 2 * SUBLANES - FOX_HEADS), (0, 0)))
    to_t = lambda a: jnp.transpose(a, (0, 1, 3, 4, 2))
    ckt, cvt, cmk, cmv = to_t(cache_fox_k), to_t(cache_fox_v), to_t(cache_mem_k), to_t(cache_mem_v)
    tms = min(TM, ms)

    fk_p, fv_p, fl_p, gs_p, mk_all, mv_all = [], [], [], [], [], []
    fk_s, fv_s, fl_s, gs_s = [], [], [], []
    for l in range(DEPTH):
        i = l // 2
        g_mix = norm_mix[l][None, :]
        mqg = jnp.tile(mem_q_norm[l], MEM_HEADS)[None, :]
        mkg = jnp.tile(mem_k_norm[l], MEM_HEADS)[None, :]
        mk_p, mv_p = _mem_kv(mem2, norm_mem[l][None, :], mem_w_kv[l].astype(BF16), mkg)
        mk_all.append(mk_p.reshape(nb, N_MEM, MEM_HEADS, MEM_HEAD_DIM))
        mv_all.append(mv_p.reshape(nb, N_MEM, MEM_HEADS, MEM_HEAD_DIM))
        mk_p = mk_p.reshape(nb, N_MEM, MEM_WIDTH)
        mv_p = mv_p.reshape(nb, N_MEM, MEM_WIDTH)
        if l % 2 == 0:
            w, wa2, ba, br, og, wo, wm = _gla_weights(gla_w_in[i], gla_w_a2[i], gla_b_a[i], gla_b_r[i],
                                                      gla_o_norm[i], gla_w_out[i])
            outs = []
            for x2, bsz, t, tm, s0, nbat in ((yp, nb, seq, TM, None, nb),
                                             (ys, ns, tdec, tms, state_gla[i], math.gcd(ns, GLA_DECODE_BATCH))):
                q, k, v, r, xq, la = _gla_proj(x2, g_mix, w, wa2, ba, tm)
                r3 = lambda a: a.reshape(bsz, t, a.shape[-1])
                o, s_new = _gla_mix(r3(q), r3(k), r3(la), r3(v), r3(r), og, br, s0,
                                    chunk=GLA_CHUNK_TOKENS if t >= GLA_CHUNK_TOKENS else 2 * SUBLANES,
                                    tblk=GLA_BLOCK_TOKENS, nbatch=nbat)
                outs.append((o.reshape(bsz * t, -1), r3(xq), s_new))
            (o_p, xq_p, sp), (o_s, xq_s, ss) = outs
            gs_p.append(sp)
            gs_s.append(ss)
        else:
            w, bfp, qg, kg, wo, wm = _fox_weights(fox_w_in[i], fox_b_f[i], fox_q_norm[i], fox_k_norm[i], fox_w_out[i])
            unpad = lambda a: a.reshape(a.shape[0], FOX_HEADS, LANES)[:, :, :FOX_HEAD_DIM]
            qa, kp, vf, vt, xq_p, lf = _fox_proj(yp, g_mix, w, qg, kg, bfp, TM)
            r3 = lambda a: a.reshape(nb, seq, a.shape[-1])
            ka = _key_bias(r3(lf), r3(kp), CUMSUM_ROWS)
            o_p = _fox_flash(r3(qa), ka, vt, min(FOX_TQ, seq), min(FOX_TK, seq)).reshape(mp, TOK_WIDTH)
            xq_p = r3(xq_p)
            fk_p.append(unpad(kp).reshape(nb, seq, FOX_HEADS, FOX_HEAD_DIM))
            fv_p.append(vf.reshape(nb, seq, FOX_HEADS, FOX_HEAD_DIM))
            fl_p.append(lf[:, :FOX_HEADS].reshape(nb, seq, FOX_HEADS))
            qa, kp, vf, _, xq_s, lf = _fox_proj(ys, g_mix, w, qg, kg, bfp, tms)
            r3 = lambda a: a.reshape(ns, tdec, a.shape[-1])
            kf = unpad(kp).reshape(ms, TOK_WIDTH)
            o_s = _fox_decode(r3(unpad(qa).astype(F32).reshape(ms, TOK_WIDTH)), r3(kf), r3(vf), r3(lf), ckt, cvt, clt,
                              page_table, i, FOX_PAGES_PER_STEP).reshape(ms, TOK_WIDTH)
            xq_s = r3(xq_s)
            fk_s.append(kf.reshape(ns, tdec, FOX_HEADS, FOX_HEAD_DIM))
            fv_s.append(vf.reshape(ns, tdec, FOX_HEADS, FOX_HEAD_DIM))
            fl_s.append(lf[:, :FOX_HEADS].reshape(ns, tdec, FOX_HEADS))
        xo_p = _mem_attn_prompt(xq_p, mk_p, mv_p, mqg, TM).reshape(mp, MEM_WIDTH)
        xo_s = _mem_attn_decode(xq_s, cmk, cmv, mqg, l, math.gcd(ns, MEM_DECODE_BATCH)).reshape(ms, MEM_WIDTH)
        g_mlp = norm_mlp[l][None, :]
        wu = mlp_w_up[l].astype(BF16)
        wd = mlp_w_down[l].astype(BF16)
        yp = _out_mlp(yp, o_p, xo_p, wo, wm, g_mlp, wu, wd, TM)
        ys = _out_mlp(ys, o_s, xo_s, wo, wm, g_mlp, wu, wd, tms)
    return (yp.reshape(nb, seq, D_MODEL), ys.reshape(ns, tdec, D_MODEL),
            jnp.stack(fk_p, axis=1), jnp.stack(fv_p, axis=1), jnp.stack(fl_p, axis=1),
            jnp.stack(gs_p, axis=0), jnp.stack(mk_all, axis=0), jnp.stack(mv_all, axis=0),
            jnp.stack(fk_s, axis=1), jnp.stack(fv_s, axis=1), jnp.stack(fl_s, axis=1),
            jnp.stack(gs_s, axis=0))
```

```python
import functools
import math

import numpy as np
import jax
import jax.numpy as jnp
from jax import lax
from jax.experimental import pallas as pl
from jax.experimental.pallas import tpu as pltpu

F32 = jnp.float32
BF16 = jnp.bfloat16

D_MODEL = 1024
DEPTH = 4
PAGE_SIZE = 128
TOK_WIDTH = 768
MEM_WIDTH = 256
N_MEM = 256
MEM_HEADS = 4
MEM_HEAD_DIM = 64
GLA_HEADS = 4
GLA_DV = 192
GLA_DK = 96
GLA_GATE_RANK = 16
GLA_TAU = 16.0
FOX_HEAD_DIM = 64
FOX_HEADS = 12
D_FF = 4 * D_MODEL
EPS = 1e-6

LANES = 128
SUBLANES = 8
GLA_DK_PAD = 128
GLA_DV_PAD = 256
NEG_BIG = -1e30
VMEM_LIMIT = 56 * 1024 * 1024


def _cparams(sem):
    return pltpu.CompilerParams(dimension_semantics=sem, vmem_limit_bytes=VMEM_LIMIT)


def _resident(a):
    return pl.BlockSpec(a.shape, lambda i: (0,) * a.ndim, pipeline_mode=pl.Buffered(1))


def _dot(a, b):
    return jnp.dot(a, b, preferred_element_type=F32)


def _dot_nt(a, b):
    return lax.dot_general(a, b, (((1,), (1,)), ((), ())), preferred_element_type=F32)


def _dot_tn(a, b):
    return lax.dot_general(a, b, (((0,), (0,)), ((), ())), preferred_element_type=F32)


def _split3(x):
    hi = x.astype(BF16)
    r1 = x - hi.astype(F32)
    mid = r1.astype(BF16)
    lo = (r1 - mid.astype(F32)).astype(BF16)
    return hi, mid, lo


def _dot01(m01, x):
    hi, mid, lo = _split3(x)
    return (_dot(m01, lo) + _dot(m01, mid)) + _dot(m01, hi)


def _rms(x, g):
    return x * lax.rsqrt(jnp.mean(x * x, axis=-1, keepdims=True) + EPS) * g


def _log_sigmoid(x):
    return jnp.minimum(x, 0.0) - jnp.log(1.0 + jnp.exp(-jnp.abs(x)))


def _half_mask(shape):
    lane = lax.broadcasted_iota(jnp.int32, shape, len(shape) - 1)
    return (lane & FOX_HEAD_DIM) == 0


def _log2(n):
    k = int(n).bit_length() - 1
    assert 1 << k == n, n
    return k


def _ones_where(cond):
    return jnp.where(cond, 1.0, 0.0).astype(BF16)


def _head_rms_pairs(x, gain):
    n = x.shape[-1] // LANES
    outs = []
    for c in range(n):
        xc = x[:, c * LANES:(c + 1) * LANES]
        sq = xc * xc
        even = _half_mask(xc.shape)
        s_e = jnp.sum(jnp.where(even, sq, 0.0), axis=-1, keepdims=True)
        s_o = jnp.sum(jnp.where(even, 0.0, sq), axis=-1, keepdims=True)
        rs = jnp.where(even, lax.rsqrt(s_e / FOX_HEAD_DIM + EPS), lax.rsqrt(s_o / FOX_HEAD_DIM + EPS))
        outs.append(xc * rs)
    return jnp.concatenate(outs, axis=-1) * gain


GLA_COLS = (GLA_HEADS * GLA_DK_PAD, GLA_HEADS * GLA_DK_PAD, GLA_HEADS * GLA_DV_PAD,
            GLA_HEADS * GLA_DV_PAD, MEM_WIDTH, LANES)
GLA_OFF = tuple(int(v) for v in np.cumsum((0,) + GLA_COLS))


def _gla_proj_kernel(x_ref, g_ref, w_ref, wa_ref, ba_ref, q_ref, k_ref, v_ref, r_ref, xq_ref, la_ref):
    xn = _rms(x_ref[...], g_ref[...]).astype(BF16)
    y = _dot(xn, w_ref[...])
    o = GLA_OFF
    q_ref[...] = y[:, o[0]:o[1]] * (GLA_DK ** -0.5)
    k_ref[...] = y[:, o[1]:o[2]]
    v_ref[...] = y[:, o[2]:o[3]].astype(v_ref.dtype)
    r_ref[...] = y[:, o[3]:o[4]]
    xq_ref[...] = y[:, o[4]:o[5]]
    a = _dot(y[:, o[5]:o[6]].astype(BF16), wa_ref[...]) + ba_ref[...]
    la_ref[...] = _log_sigmoid(a) * (1.0 / GLA_TAU)


def _gla_proj(x, g, w, wa, ba, tm):
    m = x.shape[0]
    row = lambda n: pl.BlockSpec((tm, n), lambda i: (i, 0))
    full = _resident
    hk, hv = GLA_HEADS * GLA_DK_PAD, GLA_HEADS * GLA_DV_PAD
    out_shape = (jax.ShapeDtypeStruct((m, hk), F32), jax.ShapeDtypeStruct((m, hk), F32),
                 jax.ShapeDtypeStruct((m, hv), F32), jax.ShapeDtypeStruct((m, hv), F32),
                 jax.ShapeDtypeStruct((m, MEM_WIDTH), F32), jax.ShapeDtypeStruct((m, hk), F32))
    return pl.pallas_call(
        _gla_proj_kernel,
        grid=(m // tm,),
        in_specs=[row(D_MODEL), full(g), full(w), full(wa), full(ba)],
        out_specs=(row(hk), row(hk), row(hv), row(hv), row(MEM_WIDTH), row(hk)),
        out_shape=out_shape,
        compiler_params=_cparams(("parallel",)),
    )(x, g, w, wa, ba)


FOX_PAD_WIDTH = FOX_HEADS * LANES
FOX_AUX = 3
LOG2E = 1.4426950408889634
FOX_COLS = (FOX_PAD_WIDTH, FOX_PAD_WIDTH, TOK_WIDTH, MEM_WIDTH, LANES)
FOX_OFF = tuple(int(v) for v in np.cumsum((0,) + FOX_COLS))


def _head_rms_padded(x, gain):
    outs = []
    for h in range(x.shape[-1] // LANES):
        xh = x[:, h * LANES:(h + 1) * LANES]
        ms = jnp.sum(xh * xh, axis=-1, keepdims=True) * (1.0 / FOX_HEAD_DIM)
        outs.append(xh * lax.rsqrt(ms + EPS))
    return jnp.concatenate(outs, axis=-1) * gain


def _fox_proj_kernel(x_ref, g_ref, w_ref, qg_ref, kg_ref, bf_ref, qa_ref, kp_ref, vf_ref, vt_ref, xq_ref, lf_ref):
    xn = _rms(x_ref[...], g_ref[...]).astype(BF16)
    y = _dot(xn, w_ref[...])
    o = FOX_OFF
    q = _head_rms_padded(y[:, o[0]:o[1]], qg_ref[...]) * (FOX_HEAD_DIM ** -0.5 * LOG2E)
    lane = lax.broadcasted_iota(jnp.int32, q.shape, 1) & (LANES - 1)
    aux = jnp.logical_and(lane >= FOX_HEAD_DIM, lane < FOX_HEAD_DIM + FOX_AUX)
    qa_ref[...] = jnp.where(aux, 1.0, q).astype(BF16)
    kp_ref[...] = _head_rms_padded(y[:, o[1]:o[2]], kg_ref[...])
    v = y[:, o[2]:o[3]]
    vf_ref[...] = v
    vt_ref[...] = v.T.astype(BF16)
    xq_ref[...] = y[:, o[3]:o[4]]
    lf_ref[...] = _log_sigmoid(y[:, o[4]:o[5]] + bf_ref[...])


def _fox_proj(x, g, w, qg, kg, bfp, tm):
    m = x.shape[0]
    row = lambda n: pl.BlockSpec((tm, n), lambda i: (i, 0))
    full = _resident
    tw, pw = TOK_WIDTH, FOX_PAD_WIDTH
    out_shape = (jax.ShapeDtypeStruct((m, pw), BF16), jax.ShapeDtypeStruct((m, pw), F32),
                 jax.ShapeDtypeStruct((m, tw), F32), jax.ShapeDtypeStruct((tw, m), BF16),
                 jax.ShapeDtypeStruct((m, MEM_WIDTH), F32), jax.ShapeDtypeStruct((m, LANES), F32))
    return pl.pallas_call(
        _fox_proj_kernel,
        grid=(m // tm,),
        in_specs=[row(D_MODEL), full(g), full(w), full(qg), full(kg), full(bfp)],
        out_specs=(row(pw), row(pw), row(tw), pl.BlockSpec((tw, tm), lambda i: (0, i)), row(MEM_WIDTH), row(LANES)),
        out_shape=out_shape,
        compiler_params=_cparams(("parallel",)),
    )(x, g, w, qg, kg, bfp)


def _mem_kv_kernel(x_ref, g_ref, w_ref, kg_ref, k_ref, v_ref):
    xn = _rms(x_ref[...], g_ref[...]).astype(BF16)
    y = _dot(xn, w_ref[...])
    k_ref[...] = _head_rms_pairs(y[:, :MEM_WIDTH], kg_ref[...])
    v_ref[...] = y[:, MEM_WIDTH:]


def _mem_kv(x, g, w, kg):
    m = x.shape[0]
    full = _resident
    out = jax.ShapeDtypeStruct((m, MEM_WIDTH), F32)
    return pl.pallas_call(
        _mem_kv_kernel,
        grid=(1,),
        in_specs=[full(x), full(g), full(w), full(kg)],
        out_specs=(pl.BlockSpec((m, MEM_WIDTH), lambda i: (0, 0)),) * 2,
        out_shape=(out, out),
        compiler_params=_cparams(("arbitrary",)),
    )(x, g, w, kg)


def _gla_levels(c):
    out, h = [], c // 2
    while h >= SUBLANES:
        out.append(h)
        h //= 2
    return out


def _gla_cum_matrix(c):
    t = np.arange(c)
    blocks = [(t[None, :] <= t[:, None])]
    for h in _gla_levels(c):
        anchor = (t // (2 * h)) * (2 * h) + h - 1
        blocks.append(t[None, :] <= anchor[:, None])
    return jnp.asarray(np.concatenate(blocks, axis=0).astype(np.float32), dtype=BF16)


def _gla_kernel(*refs, chunk, n_sub, t_valid, has_s0, nbatch):
    if has_s0:
        cum_ref, q_ref, k_ref, la_ref, v_ref, r_ref, og_ref, br_ref, s0_ref, o_ref, sout_ref, st_ref, pad_ref = refs
    else:
        cum_ref, q_ref, k_ref, la_ref, v_ref, r_ref, og_ref, br_ref, o_ref, sout_ref, st_ref, pad_ref = refs
        s0_ref = None
    c = chunk
    levels = _gla_levels(c)
    nb = c // SUBLANES
    step = pl.program_id(1)
    chains = [(bi, h) for bi in range(nbatch) for h in range(GLA_HEADS)]

    @pl.when(step == 0)
    def _():
        for ci, (bi, h) in enumerate(chains):
            if has_s0:
                pad_ref[...] = jnp.zeros(pad_ref.shape, F32)
                pad_ref[:GLA_DK, :GLA_DV] = s0_ref[bi, h]
                st_ref[ci] = pad_ref[...].T
            else:
                st_ref[ci] = jnp.zeros(st_ref.shape[1:], F32)

    def load(ref, bi, rows, h, width):
        x = ref[bi, rows, h * width:(h + 1) * width]
        if t_valid < c:
            x = jnp.concatenate([x, jnp.zeros((c - t_valid, width), F32)], axis=0)
        return x

    def do_chunk(ci, bi, h, rows):
        q = load(q_ref, bi, rows, h, GLA_DK_PAD)
        k = load(k_ref, bi, rows, h, GLA_DK_PAD)
        la = load(la_ref, bi, rows, h, GLA_DK_PAD)
        v = load(v_ref, bi, rows, h, GLA_DV_PAD)
        vb = v.astype(BF16)
        cum = _dot01(cum_ref[...], la)
        b = cum[:c]
        b_last = b[c - 1:c, :]
        st = st_ref[ci]
        o = _dot_nt((q * jnp.exp(b)).astype(BF16), st.astype(BF16))
        if levels:
            ti = lax.broadcasted_iota(jnp.int32, (c, c), 0)
            si = lax.broadcasted_iota(jnp.int32, (c, c), 1)
            row = lax.broadcasted_iota(jnp.int32, (c, GLA_DK_PAD), 0)
            attn = jnp.zeros((c, c), F32)
            for li, half in enumerate(levels):
                anchor = cum[(li + 1) * c:(li + 2) * c]
                gdec = jnp.exp(-jnp.abs(b - anchor))
                first = (row & half) == 0
                ql = jnp.where(first, 0.0, q * gdec).astype(BF16)
                kl = jnp.where(first, k * gdec, 0.0).astype(BF16)
                a = _dot_nt(ql, kl)
                sh = _log2(2 * half)
                attn = attn + jnp.where((ti >> sh) == (si >> sh), a, 0.0)
        else:
            ti = lax.broadcasted_iota(jnp.int32, (c, c), 0)
            si = lax.broadcasted_iota(jnp.int32, (c, c), 1)
            attn = jnp.zeros((c, c), F32)
        q3 = q.reshape(nb, SUBLANES, GLA_DK_PAD)
        k3 = k.reshape(nb, SUBLANES, GLA_DK_PAD)
        b3 = b.reshape(nb, SUBLANES, GLA_DK_PAD)
        sub = lax.broadcasted_iota(jnp.int32, (nb, SUBLANES, GLA_DK_PAD), 1)
        rel = si - (ti & -SUBLANES)
        for j in range(SUBLANES):
            kj = jnp.broadcast_to(k3[:, j:j + 1, :], k3.shape)
            bj = jnp.broadcast_to(b3[:, j:j + 1, :], b3.shape)
            e = jnp.exp(jnp.where(sub >= j, b3 - bj, -jnp.inf))
            col = jnp.sum(q3 * kj * e, axis=-1, keepdims=True).reshape(c, 1)
            attn = jnp.where(rel == j, col, attn)
        o = o + _dot(attn.astype(BF16), vb)
        kg = (k * jnp.exp(b_last - b)).astype(BF16)
        st_new = st * jnp.exp(b_last) + _dot_tn(vb, kg)
        ms = jnp.sum(o * o, axis=-1, keepdims=True) * (1.0 / GLA_DV)
        on = o * lax.rsqrt(ms + EPS) * og_ref[...]
        r = load(r_ref, bi, rows, h, GLA_DV_PAD) + br_ref[:, h * GLA_DV_PAD:(h + 1) * GLA_DV_PAD]
        res = on * (r * (1.0 / (1.0 + jnp.exp(-r))))
        return (res[:t_valid] if t_valid < c else res), st_new

    def all_chains(rows):
        outs = [do_chunk(ci, bi, h, rows) for ci, (bi, h) in enumerate(chains)]
        for ci, (bi, h) in enumerate(chains):
            res, st_new = outs[ci]
            o_ref[bi, rows, h * GLA_DV_PAD:(h + 1) * GLA_DV_PAD] = res.astype(o_ref.dtype)
            st_ref[ci] = st_new

    if n_sub == 1:
        all_chains(slice(None))
    else:
        def body(i, carry):
            all_chains(pl.ds(pl.multiple_of(i * c, c), c))
            return carry
        lax.fori_loop(0, n_sub, body, 0)

    @pl.when(step == pl.num_programs(1) - 1)
    def _():
        for ci, (bi, h) in enumerate(chains):
            pad_ref[...] = st_ref[ci].T
            sout_ref[bi, h] = pad_ref[:GLA_DK, :GLA_DV]


def _gla_mix(q, k, la, v, r, og, br, s0, *, chunk, tblk, nbatch):
    bsz, t, _ = q.shape
    has_s0 = s0 is not None
    if t < chunk:
        t_valid, n_sub, tblk = t, 1, t
    else:
        t_valid, n_sub = chunk, tblk // chunk
    cum = _gla_cum_matrix(chunk)
    hk, hv = GLA_HEADS * GLA_DK_PAD, GLA_HEADS * GLA_DV_PAD
    kblk = pl.BlockSpec((nbatch, tblk, hk), lambda b, i: (b, i, 0))
    vblk = pl.BlockSpec((nbatch, tblk, hv), lambda b, i: (b, i, 0))
    sblk = pl.BlockSpec((nbatch, GLA_HEADS, GLA_DK, GLA_DV), lambda b, i: (b, 0, 0, 0))
    in_specs = [pl.BlockSpec(cum.shape, lambda b, i: (0, 0)), kblk, kblk, kblk, vblk, vblk,
                pl.BlockSpec((1, GLA_DV_PAD), lambda b, i: (0, 0)),
                pl.BlockSpec((1, hv), lambda b, i: (0, 0))]
    args = [cum, q, k, la, v, r, og, br]
    if has_s0:
        in_specs.append(sblk)
        args.append(s0)
    kern = functools.partial(_gla_kernel, chunk=chunk, n_sub=n_sub, t_valid=t_valid, has_s0=has_s0, nbatch=nbatch)
    return pl.pallas_call(
        kern,
        grid=(bsz // nbatch, t // tblk),
        in_specs=in_specs,
        out_specs=(vblk, sblk),
        out_shape=(jax.ShapeDtypeStruct((bsz, t, hv), BF16 if t >= chunk else F32),
                   jax.ShapeDtypeStruct((bsz, GLA_HEADS, GLA_DK, GLA_DV), F32)),
        scratch_shapes=[pltpu.VMEM((nbatch * GLA_HEADS, GLA_DV_PAD, GLA_DK_PAD), F32),
                        pltpu.VMEM((GLA_DK_PAD, GLA_DV_PAD), F32)],
        compiler_params=_cparams(("parallel", "arbitrary")),
    )(*args)


def _key_bias_kernel(tri_ref, lf_ref, kp_ref, ka_ref, carry_ref):
    @pl.when(pl.program_id(1) == 0)
    def _():
        carry_ref[...] = jnp.zeros(carry_ref.shape, F32)
    c = _dot01(tri_ref[...], lf_ref[0]) + carry_ref[...]
    carry_ref[...] = c[c.shape[0] - 1:, :]
    nbias = c * (-LOG2E)
    hlane = lax.broadcasted_iota(jnp.int32, c.shape, 1)
    lane = lax.broadcasted_iota(jnp.int32, c.shape, 1)
    for h in range(FOX_HEADS):
        col = jnp.sum(jnp.where(hlane == h, nbias, 0.0), axis=-1, keepdims=True)
        hi = col.astype(BF16).astype(F32)
        mid = (col - hi).astype(BF16).astype(F32)
        lo = col - hi - mid
        kh = kp_ref[0, :, h * LANES:(h + 1) * LANES]
        kh = jnp.where(lane == FOX_HEAD_DIM, hi, kh)
        kh = jnp.where(lane == FOX_HEAD_DIM + 1, mid, kh)
        kh = jnp.where(lane == FOX_HEAD_DIM + 2, lo, kh)
        ka_ref[0, :, h * LANES:(h + 1) * LANES] = kh.astype(BF16)


def _key_bias(lf, kp, tc):
    bsz, t, n = lf.shape
    tri = jnp.asarray(np.tril(np.ones((tc, tc), np.float32)), dtype=BF16)
    kblk = pl.BlockSpec((1, tc, FOX_PAD_WIDTH), lambda b, i: (b, i, 0))
    return pl.pallas_call(
        _key_bias_kernel,
        grid=(bsz, t // tc),
        in_specs=[pl.BlockSpec((tc, tc), lambda b, i: (0, 0)), pl.BlockSpec((1, tc, n), lambda b, i: (b, i, 0)), kblk],
        out_specs=kblk,
        out_shape=jax.ShapeDtypeStruct((bsz, t, FOX_PAD_WIDTH), BF16),
        scratch_shapes=[pltpu.VMEM((1, n), F32)],
        compiler_params=_cparams(("parallel", "arbitrary")),
    )(tri, lf, kp)


def _fox_flash_kernel(qi_ref, kj_ref, q_ref, k_ref, vt_ref, o_ref, m_ref, l_ref, acc_ref, *, tq, tk):
    qi = qi_ref[pl.program_id(2)]
    kj = kj_ref[pl.program_id(2)]
    last = ((qi + 1) * tq - 1) // tk

    @pl.when(kj == 0)
    def _():
        m_ref[...] = jnp.full(m_ref.shape, NEG_BIG, F32)
        l_ref[...] = jnp.zeros(l_ref.shape, F32)
        acc_ref[...] = jnp.zeros(acc_ref.shape, F32)

    def update(masked, nk):
        vt = vt_ref[:, :nk]
        if masked:
            keyi = kj * tk + lax.broadcasted_iota(jnp.int32, (nk, tq), 0)
            qryi = qi * tq + lax.broadcasted_iota(jnp.int32, (nk, tq), 1)
            keep = keyi <= qryi
        for hh in range(2):
            st = _dot_nt(k_ref[0, :nk, hh * LANES:(hh + 1) * LANES], q_ref[0, :, hh * LANES:(hh + 1) * LANES])
            if masked:
                st = jnp.where(keep, st, NEG_BIG)
            m_old = m_ref[hh]
            m_new = jnp.maximum(m_old, jnp.max(st, axis=0, keepdims=True))
            alpha = jnp.exp2(m_old - m_new)
            p = jnp.exp2(st - m_new)
            l_ref[hh] = alpha * l_ref[hh] + jnp.sum(p, axis=0, keepdims=True)
            m_ref[hh] = m_new
            rows = slice(hh * FOX_HEAD_DIM, (hh + 1) * FOX_HEAD_DIM)
            acc_ref[rows, :] = alpha * acc_ref[rows, :] + _dot(vt, p.astype(BF16))[rows, :]

    crosses = (kj + 1) * tk - 1 > qi * tq
    visible = (qi + 1) * tq - kj * tk
    if tk % tq == 0:
        for nk in range(tq, tk + 1, tq):
            @pl.when(jnp.logical_and(crosses, visible == nk))
            def _(nk=nk):
                update(True, nk)
    else:
        @pl.when(crosses)
        def _():
            update(True, tk)

    @pl.when(jnp.logical_not(crosses))
    def _():
        update(False, tk)

    @pl.when(kj == last)
    def _():
        hd = FOX_HEAD_DIM
        ot = jnp.concatenate([acc_ref[:hd, :] / l_ref[0], acc_ref[hd:, :] / l_ref[1]], axis=0)
        o_ref[0] = ot.T.astype(o_ref.dtype)


def _fox_flash(q, k, vt, tq, tk):
    bsz, t, _ = q.shape
    npair = FOX_HEADS // 2
    pairs = [(i, j) for i in range(t // tq) for j in range(((i + 1) * tq - 1) // tk + 1)]
    qi_tbl = jnp.asarray([i for i, _ in pairs], jnp.int32)
    kj_tbl = jnp.asarray([j for _, j in pairs], jnp.int32)
    qblk = pl.BlockSpec((1, tq, 2 * LANES), lambda b, p, s, qi, kj: (b, qi[s], p))
    kblk = pl.BlockSpec((1, tk, 2 * LANES), lambda b, p, s, qi, kj: (b, kj[s], p))
    grid_spec = pltpu.PrefetchScalarGridSpec(
        num_scalar_prefetch=2,
        grid=(bsz, npair, len(pairs)),
        in_specs=[qblk, kblk, pl.BlockSpec((LANES, tk), lambda b, p, s, qi, kj: (p, b * (t // tk) + kj[s]))],
        out_specs=pl.BlockSpec((1, tq, LANES), lambda b, p, s, qi, kj: (b, qi[s], p)),
        scratch_shapes=[pltpu.VMEM((2, 1, tq), F32), pltpu.VMEM((2, 1, tq), F32), pltpu.VMEM((LANES, tq), F32)],
    )
    return pl.pallas_call(
        functools.partial(_fox_flash_kernel, tq=tq, tk=tk),
        grid_spec=grid_spec,
        out_shape=jax.ShapeDtypeStruct((bsz, t, TOK_WIDTH), BF16),
        compiler_params=_cparams(("parallel", "parallel", "arbitrary")),
    )(qi_tbl, kj_tbl, q, k, vt)


def _head_rows(x, heads):
    lane = lax.broadcasted_iota(jnp.int32, x.shape, 1) >> _log2(FOX_HEAD_DIM)
    zero = jnp.zeros(x.shape, x.dtype)
    return jnp.concatenate([jnp.where(lane == h, x, zero) for h in range(heads)], axis=0)


def _head_diag(z, heads, t):
    lane = lax.broadcasted_iota(jnp.int32, (t, z.shape[1]), 1) >> _log2(FOX_HEAD_DIM)
    out = jnp.zeros((t, z.shape[1]), z.dtype)
    for h in range(heads):
        out = jnp.where(lane == h, z[h * t:(h + 1) * t, :], out)
    return out


def _pad_rows(x, n):
    return jnp.concatenate([x, jnp.zeros((n - x.shape[0], x.shape[1]), x.dtype)], axis=0)


def _fox_decode_kernel(pt_ref, q_ref, kn_ref, vn_ref, lfn_ref, *rest, t_new, pps):
    del pt_ref
    kc_refs, vc_refs, lc_refs = rest[:pps], rest[pps:2 * pps], rest[2 * pps:3 * pps]
    o_ref, wq_ref, m_ref, l_ref, acc_ref, carry_ref, cnk_ref = rest[3 * pps:]
    p = pl.program_id(1)
    npg = pl.num_programs(1)
    heads = FOX_HEADS
    rows = heads * t_new
    tpad = 2 * SUBLANES

    @pl.when(p == 0)
    def _():
        wq_ref[...] = _head_rows(q_ref[0], heads).astype(BF16)
        m_ref[...] = jnp.full(m_ref.shape, NEG_BIG, F32)
        l_ref[...] = jnp.zeros(l_ref.shape, F32)
        acc_ref[...] = jnp.zeros(acc_ref.shape, F32)
        carry_ref[...] = jnp.zeros(carry_ref.shape, F32)
        lfn = _pad_rows(lfn_ref[0] * LOG2E, tpad)
        ti = lax.broadcasted_iota(jnp.int32, (tpad, tpad), 0)
        si = lax.broadcasted_iota(jnp.int32, (tpad, tpad), 1)
        cs = _dot01(_ones_where(si <= ti), lfn)
        rh = lax.broadcasted_iota(jnp.int32, (rows, LANES), 0) >> _log2(t_new)
        lh = lax.broadcasted_iota(jnp.int32, (rows, LANES), 1)
        hsel = _ones_where(rh == lh)
        hi, mid, lo = _split3(cs)
        cnk_ref[...] = (_dot_nt(hsel, lo) + _dot_nt(hsel, mid)) + _dot_nt(hsel, hi)

    def online(s, pv_of):
        m_old = m_ref[...]
        m_new = jnp.maximum(m_old, jnp.max(s, axis=-1, keepdims=True))
        alpha = jnp.exp2(m_old - m_new)
        pr = jnp.exp2(s - m_new)
        l_ref[...] = alpha * l_ref[...] + jnp.sum(pr, axis=-1, keepdims=True)
        m_ref[...] = m_new
        acc_ref[...] = alpha * acc_ref[...] + pv_of(pr.astype(BF16))

    cnk = cnk_ref[...]
    ri = lax.broadcasted_iota(jnp.int32, cnk.shape, 0) & (t_new - 1)
    ci = lax.broadcasted_iota(jnp.int32, cnk.shape, 1)
    cn_col = jnp.sum(jnp.where(ri == ci, cnk, 0.0), axis=-1, keepdims=True)

    ji = lax.broadcasted_iota(jnp.int32, (PAGE_SIZE, PAGE_SIZE), 0)
    si2 = lax.broadcasted_iota(jnp.int32, (PAGE_SIZE, PAGE_SIZE), 1)
    after = _ones_where(ji > si2)
    carry = carry_ref[...]
    parts = []
    for j in range(pps):
        lp = lc_refs[j][0, 0] * LOG2E
        suffix = _dot01_rhs(lp, after) + carry
        carry = carry + jnp.sum(lp, axis=-1, keepdims=True)
        r96 = jnp.concatenate([jnp.broadcast_to(suffix[h:h + 1, :], (t_new, PAGE_SIZE)) for h in range(heads)],
                              axis=0)
        kt = kc_refs[j][0, 0].reshape(TOK_WIDTH, PAGE_SIZE).astype(BF16)
        parts.append(_dot(wq_ref[...], kt) + r96)
    carry_ref[...] = carry

    def pv_pages(pr):
        out = None
        for j in range(pps):
            vt = vc_refs[j][0, 0].reshape(TOK_WIDTH, PAGE_SIZE).astype(BF16)
            pvj = _dot_nt(pr[:, j * PAGE_SIZE:(j + 1) * PAGE_SIZE], vt)
            out = pvj if out is None else out + pvj
        return out

    online(jnp.concatenate(parts, axis=1) + cn_col, pv_pages)

    @pl.when(p == npg - 1)
    def _():
        kn = _pad_rows(kn_ref[0], tpad).astype(BF16)
        vn = _pad_rows(vn_ref[0], tpad).astype(BF16)
        sn = jnp.where(ci <= ri, _dot_nt(wq_ref[...], kn) + cn_col - cnk, NEG_BIG)
        online(sn, lambda pr: _dot(pr, vn))
        o_ref[0] = _head_diag(acc_ref[...] / l_ref[...], heads, t_new).astype(o_ref.dtype)


def _dot01_rhs(x, m01):
    hi, mid, lo = _split3(x)
    return (_dot(lo, m01) + _dot(mid, m01)) + _dot(hi, m01)


def _fox_decode(q, kn, vn, lfn, cache_k, cache_v, cache_lt, page_table, layer, pps):
    nseq, t_new, _ = q.shape
    heads, hd = cache_k.shape[2], cache_k.shape[3]
    npg = page_table.shape[1]
    pps = math.gcd(npg, pps)
    rows = heads * t_new
    tpad = 2 * SUBLANES
    seq = lambda n: pl.BlockSpec((1, t_new, n), lambda b, p, pt: (b, 0, 0))

    def page_idx(j, ndim):
        return lambda b, p, pt: (pt[b, npg - 1 - (p * pps + j)], layer) + (0,) * (ndim - 2)

    kv_specs = [pl.BlockSpec((1, 1, heads, hd, PAGE_SIZE), page_idx(j, 5)) for j in range(pps)]
    lt_specs = [pl.BlockSpec((1, 1, tpad, PAGE_SIZE), page_idx(j, 4)) for j in range(pps)]
    grid_spec = pltpu.PrefetchScalarGridSpec(
        num_scalar_prefetch=1,
        grid=(nseq, npg // pps),
        in_specs=[seq(TOK_WIDTH), seq(TOK_WIDTH), seq(TOK_WIDTH), seq(LANES)] + kv_specs + kv_specs + lt_specs,
        out_specs=seq(TOK_WIDTH),
        scratch_shapes=[pltpu.VMEM((rows, TOK_WIDTH), BF16), pltpu.VMEM((rows, 1), F32),
                        pltpu.VMEM((rows, 1), F32), pltpu.VMEM((rows, TOK_WIDTH), F32),
                        pltpu.VMEM((tpad, 1), F32), pltpu.VMEM((rows, tpad), F32)],
    )
    return pl.pallas_call(
        functools.partial(_fox_decode_kernel, t_new=t_new, pps=pps),
        grid_spec=grid_spec,
        out_shape=jax.ShapeDtypeStruct((nseq, t_new, TOK_WIDTH), F32),
        compiler_params=_cparams(("parallel", "arbitrary")),
    )(page_table, q, kn, vn, lfn, *([cache_k] * pps), *([cache_v] * pps), *([cache_lt] * pps))


def _mem_attn_prompt_kernel(xq_ref, mk_ref, mv_ref, g_ref, o_ref):
    q = _head_rms_pairs(xq_ref[0], g_ref[...]) * (MEM_HEAD_DIM ** -0.5)
    outs = []
    for pr in range(MEM_WIDTH // LANES):
        cols = slice(pr * LANES, (pr + 1) * LANES)
        qp = q[:, cols].astype(BF16)
        kp = mk_ref[0][:, cols].astype(BF16)
        vp = mv_ref[0][:, cols].astype(BF16)
        even = _half_mask(qp.shape)
        zero = jnp.zeros(qp.shape, BF16)
        res = []
        for qh in (jnp.where(even, qp, zero), jnp.where(even, zero, qp)):
            s = _dot_nt(qh, kp)
            e = jnp.exp(s - jnp.max(s, axis=-1, keepdims=True))
            pv = _dot(e.astype(BF16), vp)
            res.append(pv / jnp.sum(e, axis=-1, keepdims=True))
        outs.append(jnp.where(even, res[0], res[1]))
    o_ref[0] = jnp.concatenate(outs, axis=-1).astype(o_ref.dtype)


def _mem_attn_prompt(xq, mk, mv, g, tq):
    bsz, t, _ = xq.shape
    qblk = pl.BlockSpec((1, tq, MEM_WIDTH), lambda b, i: (b, i, 0))
    mblk = pl.BlockSpec((1, N_MEM, MEM_WIDTH), lambda b, i: (b, 0, 0))
    return pl.pallas_call(
        _mem_attn_prompt_kernel,
        grid=(bsz, t // tq),
        in_specs=[qblk, mblk, mblk, pl.BlockSpec((1, MEM_WIDTH), lambda b, i: (0, 0))],
        out_specs=qblk,
        out_shape=jax.ShapeDtypeStruct((bsz, t, MEM_WIDTH), BF16),
        compiler_params=_cparams(("parallel", "parallel")),
    )(xq, mk, mv, g)


def _mem_attn_decode_kernel(xq_ref, mk_ref, mv_ref, g_ref, o_ref, *, t_new, nbatch):
    for bi in range(nbatch):
        q = _head_rms_pairs(xq_ref[bi], g_ref[...]) * (MEM_HEAD_DIM ** -0.5)
        wq = _head_rows(q, MEM_HEADS).astype(BF16)
        kt = mk_ref[0, bi].reshape(MEM_WIDTH, N_MEM).astype(BF16)
        vt = mv_ref[0, bi].reshape(MEM_WIDTH, N_MEM).astype(BF16)
        s = _dot(wq, kt)
        e = jnp.exp(s - jnp.max(s, axis=-1, keepdims=True))
        z = _dot_nt(e.astype(BF16), vt) / jnp.sum(e, axis=-1, keepdims=True)
        o_ref[bi] = _head_diag(z, MEM_HEADS, t_new).astype(o_ref.dtype)


def _mem_attn_decode(xq, cache_mk, cache_mv, g, layer, nbatch):
    nseq, t_new, _ = xq.shape
    qblk = pl.BlockSpec((nbatch, t_new, MEM_WIDTH), lambda b: (b, 0, 0))
    mblk = pl.BlockSpec((1, nbatch, MEM_HEADS, MEM_HEAD_DIM, N_MEM), lambda b: (layer, b, 0, 0, 0))
    return pl.pallas_call(
        functools.partial(_mem_attn_decode_kernel, t_new=t_new, nbatch=nbatch),
        grid=(nseq // nbatch,),
        in_specs=[qblk, mblk, mblk, pl.BlockSpec((1, MEM_WIDTH), lambda b: (0, 0))],
        out_specs=qblk,
        out_shape=jax.ShapeDtypeStruct((nseq, t_new, MEM_WIDTH), F32),
        compiler_params=_cparams(("parallel",)),
    )(xq, cache_mk, cache_mv, g)


FF_CHUNK = 1024


def _out_mlp_kernel(x_ref, o_ref, xo_ref, wo_ref, wm_ref, g_ref, wu_ref, wd_ref, y_ref):
    y_ref[...] = (x_ref[...] + _dot(o_ref[...].astype(BF16), wo_ref[...])
                  + _dot(xo_ref[...].astype(BF16), wm_ref[...]))
    y1 = y_ref[...]
    xn = _rms(y1, g_ref[...]).astype(BF16)
    acc = y1
    for c in range(D_FF // FF_CHUNK):
        cols = slice(c * FF_CHUNK, (c + 1) * FF_CHUNK)
        h = jnp.maximum(_dot(xn, wu_ref[:, cols]), 0.0)
        acc = acc + _dot((h * h).astype(BF16), wd_ref[cols, :])
    y_ref[...] = acc


def _out_mlp(x, o, xo, wo, wm, g, wu, wd, tm):
    m = x.shape[0]
    row = lambda n: pl.BlockSpec((tm, n), lambda i: (i, 0))
    full = lambda a: pl.BlockSpec(a.shape, lambda i: (0,) * a.ndim, pipeline_mode=pl.Buffered(1))
    return pl.pallas_call(
        _out_mlp_kernel,
        grid=(m // tm,),
        in_specs=[row(D_MODEL), row(o.shape[1]), row(MEM_WIDTH), full(wo), full(wm), full(g), full(wu), full(wd)],
        out_specs=row(D_MODEL),
        out_shape=jax.ShapeDtypeStruct((m, D_MODEL), F32),
        compiler_params=_cparams(("parallel",)),
    )(x, o, xo, wo, wm, g, wu, wd)


def _pad_heads(w, heads, d, dpad):
    lead = w.shape[:-1]
    w = w.reshape(lead + (heads, d))
    w = jnp.pad(w, [(0, 0)] * len(lead) + [(0, 0), (0, dpad - d)])
    return w.reshape(lead + (heads * dpad,))


def _gla_weights(w_in, w_a2, b_a, b_r, o_gain, w_out):
    hk = GLA_HEADS * GLA_DK
    splits = np.cumsum((hk, hk, TOK_WIDTH, GLA_GATE_RANK, TOK_WIDTH))
    wq, wk, wv, wa, wr, wx = jnp.split(w_in, [int(s) for s in splits], axis=1)
    w = jnp.concatenate([
        _pad_heads(wq, GLA_HEADS, GLA_DK, GLA_DK_PAD), _pad_heads(wk, GLA_HEADS, GLA_DK, GLA_DK_PAD),
        _pad_heads(wv, GLA_HEADS, GLA_DV, GLA_DV_PAD), _pad_heads(wr, GLA_HEADS, GLA_DV, GLA_DV_PAD),
        wx, jnp.pad(wa, ((0, 0), (0, LANES - GLA_GATE_RANK)))], axis=1).astype(BF16)
    wa2 = jnp.pad(_pad_heads(w_a2, GLA_HEADS, GLA_DK, GLA_DK_PAD), ((0, LANES - GLA_GATE_RANK), (0, 0))).astype(BF16)
    ba = _pad_heads(b_a[None, :], GLA_HEADS, GLA_DK, GLA_DK_PAD)
    br = _pad_heads(b_r[None, :], GLA_HEADS, GLA_DV, GLA_DV_PAD)
    og = jnp.pad(o_gain[None, :], ((0, 0), (0, GLA_DV_PAD - GLA_DV)))
    wo = _pad_heads(w_out[:TOK_WIDTH].T, GLA_HEADS, GLA_DV, GLA_DV_PAD).T.astype(BF16)
    wm = w_out[TOK_WIDTH:].astype(BF16)
    return w, wa2, ba, br, og, wo, wm


def _fox_weights(w_in, b_f, q_gain, k_gain, w_out):
    splits = np.cumsum((TOK_WIDTH, TOK_WIDTH, TOK_WIDTH, FOX_HEADS))
    wq, wk, wv, wf, wx = jnp.split(w_in, [int(s) for s in splits], axis=1)
    padh = lambda a: _pad_heads(a, FOX_HEADS, FOX_HEAD_DIM, LANES)
    w = jnp.concatenate([padh(wq), padh(wk), wv, wx, jnp.pad(wf, ((0, 0), (0, LANES - FOX_HEADS)))],
                        axis=1).astype(BF16)
    bfp = jnp.pad(b_f[None, :], ((0, 0), (0, LANES - FOX_HEADS)))
    qg = padh(jnp.tile(q_gain, FOX_HEADS)[None, :])
    kg = padh(jnp.tile(k_gain, FOX_HEADS)[None, :])
    return w, bfp, qg, kg, w_out[:TOK_WIDTH].astype(BF16), w_out[TOK_WIDTH:].astype(BF16)


TM = 512
GLA_CHUNK_TOKENS = 128
GLA_BLOCK_TOKENS = 256
GLA_DECODE_BATCH = 4
MEM_DECODE_BATCH = 8
FOX_TQ = 1024
FOX_TK = 2048
FOX_PAGES_PER_STEP = 16
CUMSUM_ROWS = 512


def kernel(x_prompt, x_sample, mem_prompt, cache_fox_k, cache_fox_v, cache_fox_logf, page_table, state_gla, cache_mem_k, cache_mem_v, norm_mix, norm_mlp, norm_mem, gla_w_in, gla_w_a2, gla_b_a, gla_b_r, gla_o_norm, gla_w_out, fox_w_in, fox_b_f, fox_q_norm, fox_k_norm, fox_w_out, mem_w_kv, mem_q_norm, mem_k_norm, mlp_w_up, mlp_w_down):
    nb, seq, _ = x_prompt.shape
    ns, tdec, _ = x_sample.shape
    mp, ms = nb * seq, ns * tdec
    yp = x_prompt.reshape(mp, D_MODEL)
    ys = x_sample.reshape(ms, D_MODEL)
    mem2 = mem_prompt.reshape(nb * N_MEM, D_MODEL)
    clt = jnp.pad(jnp.swapaxes(cache_fox_logf, 2, 3), ((0, 0), (0, 0), (0, 2 * SUBLANES - FOX_HEADS), (0, 0)))
    to_t = lambda a: jnp.transpose(a, (0, 1, 3, 4, 2))
    ckt, cvt, cmk, cmv = to_t(cache_fox_k), to_t(cache_fox_v), to_t(cache_mem_k), to_t(cache_mem_v)
    tms = min(TM, ms)

    fk_p, fv_p, fl_p, gs_p, mk_all, mv_all = [], [], [], [], [], []
    fk_s, fv_s, fl_s, gs_s = [], [], [], []
    for l in range(DEPTH):
        i = l // 2
        g_mix = norm_mix[l][None, :]
        mqg = jnp.tile(mem_q_norm[l], MEM_HEADS)[None, :]
        mkg = jnp.tile(mem_k_norm[l], MEM_HEADS)[None, :]
        mk_p, mv_p = _mem_kv(mem2, norm_mem[l][None, :], mem_w_kv[l].astype(BF16), mkg)
        mk_all.append(mk_p.reshape(nb, N_MEM, MEM_HEADS, MEM_HEAD_DIM))
        mv_all.append(mv_p.reshape(nb, N_MEM, MEM_HEADS, MEM_HEAD_DIM))
        mk_p = mk_p.reshape(nb, N_MEM, MEM_WIDTH)
        mv_p = mv_p.reshape(nb, N_MEM, MEM_WIDTH)
        if l % 2 == 0:
            w, wa2, ba, br, og, wo, wm = _gla_weights(gla_w_in[i], gla_w_a2[i], gla_b_a[i], gla_b_r[i],
                                                      gla_o_norm[i], gla_w_out[i])
            outs = []
            for x2, bsz, t, tm, s0, nbat in ((yp, nb, seq, TM, None, nb),
                                             (ys, ns, tdec, tms, state_gla[i], math.gcd(ns, GLA_DECODE_BATCH))):
                q, k, v, r, xq, la = _gla_proj(x2, g_mix, w, wa2, ba, tm)
                r3 = lambda a: a.reshape(bsz, t, a.shape[-1])
                o, s_new = _gla_mix(r3(q), r3(k), r3(la), r3(v), r3(r), og, br, s0,
                                    chunk=GLA_CHUNK_TOKENS if t >= GLA_CHUNK_TOKENS else 2 * SUBLANES,
                                    tblk=GLA_BLOCK_TOKENS, nbatch=nbat)
                outs.append((o.reshape(bsz * t, -1), r3(xq), s_new))
            (o_p, xq_p, sp), (o_s, xq_s, ss) = outs
            gs_p.append(sp)
            gs_s.append(ss)
        else:
            w, bfp, qg, kg, wo, wm = _fox_weights(fox_w_in[i], fox_b_f[i], fox_q_norm[i], fox_k_norm[i], fox_w_out[i])
            unpad = lambda a: a.reshape(a.shape[0], FOX_HEADS, LANES)[:, :, :FOX_HEAD_DIM]
            qa, kp, vf, vt, xq_p, lf = _fox_proj(yp, g_mix, w, qg, kg, bfp, TM)
            r3 = lambda a: a.reshape(nb, seq, a.shape[-1])
            ka = _key_bias(r3(lf), r3(kp), CUMSUM_ROWS)
            o_p = _fox_flash(r3(qa), ka, vt, min(FOX_TQ, seq), min(FOX_TK, seq)).reshape(mp, TOK_WIDTH)
            xq_p = r3(xq_p)
            fk_p.append(unpad(kp).reshape(nb, seq, FOX_HEADS, FOX_HEAD_DIM))
            fv_p.append(vf.reshape(nb, seq, FOX_HEADS, FOX_HEAD_DIM))
            fl_p.append(lf[:, :FOX_HEADS].reshape(nb, seq, FOX_HEADS))
            qa, kp, vf, _, xq_s, lf = _fox_proj(ys, g_mix, w, qg, kg, bfp, tms)
            r3 = lambda a: a.reshape(ns, tdec, a.shape[-1])
            kf = unpad(kp).reshape(ms, TOK_WIDTH)
            o_s = _fox_decode(r3(unpad(qa).astype(F32).reshape(ms, TOK_WIDTH)), r3(kf), r3(vf), r3(lf), ckt, cvt, clt,
                              page_table, i, FOX_PAGES_PER_STEP).reshape(ms, TOK_WIDTH)
            xq_s = r3(xq_s)
            fk_s.append(kf.reshape(ns, tdec, FOX_HEADS, FOX_HEAD_DIM))
            fv_s.append(vf.reshape(ns, tdec, FOX_HEADS, FOX_HEAD_DIM))
            fl_s.append(lf[:, :FOX_HEADS].reshape(ns, tdec, FOX_HEADS))
        xo_p = _mem_attn_prompt(xq_p, mk_p, mv_p, mqg, TM).reshape(mp, MEM_WIDTH)
        xo_s = _mem_attn_decode(xq_s, cmk, cmv, mqg, l, math.gcd(ns, MEM_DECODE_BATCH)).reshape(ms, MEM_WIDTH)
        g_mlp = norm_mlp[l][None, :]
        wu = mlp_w_up[l].astype(BF16)
        wd = mlp_w_down[l].astype(BF16)
        yp = _out_mlp(yp, o_p, xo_p, wo, wm, g_mlp, wu, wd, TM)
        ys = _out_mlp(ys, o_s, xo_s, wo, wm, g_mlp, wu, wd, tms)
    return (yp.reshape(nb, seq, D_MODEL), ys.reshape(ns, tdec, D_MODEL),
            jnp.stack(fk_p, axis=1), jnp.stack(fv_p, axis=1), jnp.stack(fl_p, axis=1),
            jnp.stack(gs_p, axis=0), jnp.stack(mk_all, axis=0), jnp.stack(mv_all, axis=0),
            jnp.stack(fk_s, axis=1), jnp.stack(fv_s, axis=1), jnp.stack(fl_s, axis=1),
            jnp.stack(gs_s, axis=0))
```
